```python
import math
import jax, jax.numpy as jnp
from jax import lax
import numpy as np

D_MODEL = 2048
BATCH = 4
SEQ = 4096
DEPTH = 4

CHUNK = 64
N_MIXERS = 3
EPS = 1e-6
DA_HEAD_DIM = 128
DA_HEADS = D_MODEL // (2 * DA_HEAD_DIM)
DA_VALUE_DIM = 2 * DA_HEAD_DIM
Q_BLOCK = 128
S5_GROUP = 16
S5_GROUPS = D_MODEL // S5_GROUP
S5_STATE = 64
S5_DT_MIN = 0.001
S5_DT_MAX = 0.1
RET_QK_DIM = 256
RET_HEADS = D_MODEL // RET_QK_DIM
RET_V_DIM = 2 * RET_QK_DIM
ROPE_BASE = 10000.0
D_FF = 4 * D_MODEL
N_A = (DEPTH + N_MIXERS - 1) // N_MIXERS
N_B = (DEPTH + N_MIXERS - 2) // N_MIXERS
N_C = DEPTH // N_MIXERS

kernel_name = "hybrid_diffattn_s5_retention_sqrelu"

F32 = jnp.float32


def rmsnorm(x, g):
    xf = x.astype(F32)
    y = xf * lax.rsqrt(jnp.mean(xf * xf, axis=-1, keepdims=True) + EPS)
    return (y * g.astype(F32)).astype(x.dtype)


def rotary(t, pos):
    half = t.shape[-1] // 2
    inv = 1.0 / (ROPE_BASE ** jnp.linspace(0.0, 1.0, half, dtype=F32))
    ang = pos[:, None] * inv[None, :]
    cos = jnp.cos(ang)[None, :, None, :]
    sin = jnp.sin(ang)[None, :, None, :]
    t1, t2 = t[..., :half], t[..., half:]
    return jnp.concatenate([t1 * cos - t2 * sin, t1 * sin + t2 * cos], axis=-1)


def diff_attention(h, w_in, lam_p, subln_g, w_out, lambda_init):
    B, S, D = h.shape
    q, k, v = jnp.split(h @ w_in, 3, axis=-1)
    q = q.reshape(B, S, DA_HEADS, 2, DA_HEAD_DIM).astype(F32) * DA_HEAD_DIM ** -0.5
    k = k.reshape(B, S, DA_HEADS, 2, DA_HEAD_DIM).astype(F32)
    v = v.reshape(B, S, DA_HEADS, DA_VALUE_DIM).astype(F32)
    lp = lam_p.astype(F32)
    lam = jnp.exp(jnp.sum(lp[0] * lp[1])) - jnp.exp(jnp.sum(lp[2] * lp[3])) + lambda_init
    n_blk = S // Q_BLOCK
    k_chunk = jnp.arange(S) // CHUNK
    q_blocks = q.reshape(B, n_blk, Q_BLOCK, DA_HEADS, 2, DA_HEAD_DIM).swapaxes(0, 1)

    def block(args):
        qb, bi = args
        q_chunk = (bi * Q_BLOCK + jnp.arange(Q_BLOCK)) // CHUNK
        mask = k_chunk[None, :] <= q_chunk[:, None]
        s = jnp.einsum('bqhtd,bkhtd->bhtqk', qb, k)
        p = jax.nn.softmax(jnp.where(mask, s, -jnp.inf), axis=-1)
        pd = p[:, :, 0] - lam * p[:, :, 1]
        return jnp.einsum('bhqk,bkhe->bqhe', pd, v)

    o = lax.map(block, (q_blocks, jnp.arange(n_blk)))
    o = o.swapaxes(0, 1).reshape(B, S, DA_HEADS, DA_VALUE_DIM)
    o = o * lax.rsqrt(jnp.mean(o * o, axis=-1, keepdims=True) + EPS)
    o = o * subln_g.astype(F32) * (1.0 - lambda_init)
    return o.reshape(B, S, D).astype(h.dtype) @ w_out


def _scan_combine(first, second):
    a1r, a1i, b1r, b1i = first
    a2r, a2i, b2r, b2i = second
    ar = a2r * a1r - a2i * a1i
    ai = a2r * a1i + a2i * a1r
    br = a2r * b1r - a2i * b1i + b2r
    bi = a2r * b1i + a2i * b1r + b2i
    return (ar, ai, br, bi)


def s5_mixer(h, a_re, a_im, log_dt, b_re, b_im, c_re, c_im, d_skip, w_glu):
    B, S, D = h.shape
    lam_re = a_re.astype(F32)
    lam_im = a_im.astype(F32)
    dt = jnp.exp(log_dt.astype(F32))[:, None]
    mag = jnp.exp(lam_re * dt)
    ab_re = mag * jnp.cos(lam_im * dt)
    ab_im = mag * jnp.sin(lam_im * dt)
    den = lam_re * lam_re + lam_im * lam_im
    nr, ni = ab_re - 1.0, ab_im
    coef_re = (nr * lam_re + ni * lam_im) / den
    coef_im = (ni * lam_re - nr * lam_im) / den
    br, bim = b_re.astype(F32), b_im.astype(F32)
    bb_re = coef_re[..., None] * br - coef_im[..., None] * bim
    bb_im = coef_re[..., None] * bim + coef_im[..., None] * br
    cr, ci = c_re.astype(F32), c_im.astype(F32)
    a_seq_re = jnp.broadcast_to(ab_re, (S, S5_GROUPS, S5_STATE))
    a_seq_im = jnp.broadcast_to(ab_im, (S, S5_GROUPS, S5_STATE))
    u = h.astype(F32).reshape(B, S, S5_GROUPS, S5_GROUP)

    def per_sequence(u_s):
        bu_re = jnp.einsum('sgc,gpc->sgp', u_s, bb_re)
        bu_im = jnp.einsum('sgc,gpc->sgp', u_s, bb_im)
        _, _, x_re, x_im = lax.associative_scan(
            _scan_combine, (a_seq_re, a_seq_im, bu_re, bu_im), axis=0)
        return jnp.einsum('sgp,gcp->sgc', x_re, cr) - jnp.einsum('sgp,gcp->sgc', x_im, ci)

    y = lax.map(per_sequence, u).reshape(B, S, D)
    y = y + d_skip.astype(F32) * h.astype(F32)
    g = jax.nn.gelu(y).astype(h.dtype)
    val, gate = jnp.split(g @ w_glu, 2, axis=-1)
    return val * jax.nn.sigmoid(gate)


def retention_mixer(h, w_in, w_out):
    B, S, D = h.shape
    proj = h @ w_in
    q, k, v, g = jnp.split(proj, [D, 2 * D, 4 * D], axis=-1)
    pos = jnp.arange(S, dtype=F32)
    q = rotary(q.reshape(B, S, RET_HEADS, RET_QK_DIM).astype(F32), pos)
    k = rotary(k.reshape(B, S, RET_HEADS, RET_QK_DIM).astype(F32), pos) * RET_QK_DIM ** -0.5
    v = v.reshape(B, S, RET_HEADS, RET_V_DIM).astype(F32)
    log_gamma = jnp.log(1.0 - jnp.exp2(-5.0 - jnp.arange(RET_HEADS, dtype=F32)))
    idx = jnp.arange(CHUNK, dtype=F32)
    intra_decay = jnp.exp(log_gamma[:, None, None] * jnp.abs(idx[:, None] - idx[None, :]))
    q_decay = jnp.exp(log_gamma[None, :] * (idx[:, None] + 1.0))
    k_decay = jnp.exp(log_gamma[None, :] * (CHUNK - 1.0 - idx[:, None]))
    chunk_decay = jnp.exp(log_gamma * CHUNK)
    n_ch = S // CHUNK

    def to_chunks(t):
        return t.reshape(B, n_ch, CHUNK, *t.shape[2:]).swapaxes(0, 1)

    def step(R, inp):
        qc, kc, vc = inp
        s = jnp.einsum('bnhd,bmhd->bhnm', qc, kc) * intra_decay
        o = jnp.einsum('bhnm,bmhe->bnhe', s, vc)
        o = o + jnp.einsum('bnhd,bhde->bnhe', qc * q_decay[..., None], R)
        R = R * chunk_decay[None, :, None, None] + jnp.einsum(
            'bmhd,bmhe->bhde', kc * k_decay[..., None], vc)
        return R, o

    R0 = jnp.zeros((B, RET_HEADS, RET_QK_DIM, RET_V_DIM), F32)
    _, o = lax.scan(step, R0, (to_chunks(q), to_chunks(k), to_chunks(v)))
    o = o.swapaxes(0, 1).reshape(B, S, RET_HEADS, RET_V_DIM)
    o = o * lax.rsqrt(jnp.mean(o * o, axis=-1, keepdims=True) + EPS)
    o = (jax.nn.silu(g.astype(F32)) * o.reshape(B, S, RET_HEADS * RET_V_DIM)).astype(h.dtype)
    return o @ w_out


def sqrelu_mlp(h, w1, w2):
    a = jax.nn.relu(h @ w1)
    return (a * a) @ w2


def setup_inputs(seed: int = 0) -> dict:
    key = jax.random.key(seed)
    ks = jax.random.split(key, 24)
    D = D_MODEL

    def w(k, shape, fan_in):
        return jax.random.normal(k, shape, F32) * fan_in ** -0.5

    def gain(k, shape):
        return 1.0 + 0.02 * jax.random.normal(k, shape, F32)

    G, P, C = S5_GROUPS, S5_STATE, S5_GROUP
    return {
        "x": jax.random.normal(ks[0], (BATCH, SEQ, D), F32),
        "norm_mix": gain(ks[1], (DEPTH, D)),
        "norm_mlp": gain(ks[2], (DEPTH, D)),
        "norm_final": gain(ks[3], (D,)),
        "a_w_in": w(ks[4], (N_A, D, 3 * D), D),
        "a_lambda": 0.1 * jax.random.normal(ks[5], (N_A, 4, DA_HEAD_DIM), F32),
        "a_subln": gain(ks[6], (N_A, DA_VALUE_DIM)),
        "a_w_out": w(ks[7], (N_A, D, D), D),
        "b_a_re": -0.5 + 0.01 * jax.random.normal(ks[8], (N_B, G, P), F32),
        "b_a_im": jnp.pi * jnp.arange(P, dtype=F32)[None, None, :]
                  + 0.01 * jax.random.normal(ks[9], (N_B, G, P), F32),
        "b_log_dt": jax.random.uniform(ks[10], (N_B, G), F32,
                                       math.log(S5_DT_MIN), math.log(S5_DT_MAX)),
        "b_b_re": w(ks[11], (N_B, G, P, C), 2 * C),
        "b_b_im": w(ks[12], (N_B, G, P, C), 2 * C),
        "b_c_re": w(ks[13], (N_B, G, C, P), 2 * P),
        "b_c_im": w(ks[14], (N_B, G, C, P), 2 * P),
        "b_d": jax.random.normal(ks[15], (N_B, D), F32),
        "b_w_glu": w(ks[16], (N_B, D, 2 * D), D),
        "c_w_in": w(ks[17], (N_C, D, 6 * D), D),
        "c_w_out": w(ks[18], (N_C, 2 * D, D), 2 * D),
        "mlp_w1": w(ks[19], (DEPTH, D, D_FF), D),
        "mlp_w2": w(ks[20], (DEPTH, D_FF, D), D_FF),
    }


def reference(x, norm_mix, norm_mlp, norm_final, a_w_in, a_lambda, a_subln, a_w_out,
              b_a_re, b_a_im, b_log_dt, b_b_re, b_b_im, b_c_re, b_c_im, b_d, b_w_glu,
              c_w_in, c_w_out, mlp_w1, mlp_w2):
    for i in range(DEPTH):
        kind = i % N_MIXERS
        j = i // N_MIXERS
        h = rmsnorm(x, norm_mix[i])
        if kind == 0:
            lambda_init = 0.8 - 0.6 * math.exp(-0.3 * i)
            mix = diff_attention(h, a_w_in[j], a_lambda[j], a_subln[j], a_w_out[j], lambda_init)
        elif kind == 1:
            mix = s5_mixer(h, b_a_re[j], b_a_im[j], b_log_dt[j], b_b_re[j], b_b_im[j],
                           b_c_re[j], b_c_im[j], b_d[j], b_w_glu[j])
        else:
            mix = retention_mixer(h, c_w_in[j], c_w_out[j])
        x = x + mix
        x = x + sqrelu_mlp(rmsnorm(x, norm_mlp[i]), mlp_w1[i], mlp_w2[i])
    return rmsnorm(x, norm_final)
```

```python
import math
from functools import partial

import jax
import jax.numpy as jnp
from jax import lax
from jax.experimental import pallas as pl
from jax.experimental.pallas import tpu as pltpu

F32 = jnp.float32
BF16 = jnp.bfloat16

EPS = 1e-6
CHUNK = 64
N_MIXERS = 3
DA_HEAD_DIM = 128
DA_VALUE_DIM = 2 * DA_HEAD_DIM
RET_QK_DIM = 256
RET_V_DIM = 2 * RET_QK_DIM
ROPE_BASE = 10000.0
S5_GROUP = 16
S5_STATE = 64

LANES = 128
SUBLANES = 8
VMEM_LIMIT_BYTES = 56 * 1024 * 1024
LOG2E = math.log2(math.e)
NEG_BIG = -1e30

NORM_ROWS = 256
S5_SUB = 8
S5_SEQS = SUBLANES


def _params(semantics):
    return pltpu.CompilerParams(dimension_semantics=semantics, vmem_limit_bytes=VMEM_LIMIT_BYTES)


def _fit(n, preferred):
    t = min(preferred, n)
    while n % t:
        t //= 2
    return t


def _rmsnorm(x, g):
    y = x * lax.rsqrt(jnp.mean(x * x, axis=-1, keepdims=True) + EPS)
    return y * g


def _norm_block_to(x_ref, g_ref, dst_ref):
    rows_total = x_ref.shape[0]
    step = min(NORM_ROWS, rows_total)
    g = g_ref[...]

    def body(r, carry):
        rows = pl.ds(pl.multiple_of(r * step, step), step)
        dst_ref[rows, :] = _rmsnorm(x_ref[rows, :], g).astype(dst_ref.dtype)
        return carry

    lax.fori_loop(0, rows_total // step, body, 0)


def _norm_matmul_kernel(x_ref, g_ref, w_ref, o_ref, hn_ref):
    @pl.when(pl.program_id(1) == 0)
    def _():
        _norm_block_to(x_ref, g_ref, hn_ref)

    o_ref[...] = jnp.dot(hn_ref[...], w_ref[...], preferred_element_type=F32).astype(o_ref.dtype)


def norm_matmul(x, g, w, *, tm=1024, tn=1024, name):
    m, d = x.shape
    n = w.shape[1]
    tm, tn = _fit(m, tm), _fit(n, tn)
    return pl.pallas_call(
        _norm_matmul_kernel,
        out_shape=jax.ShapeDtypeStruct((m, n), BF16),
        grid=(m // tm, n // tn),
        in_specs=[
            pl.BlockSpec((tm, d), lambda i, j: (i, 0)),
            pl.BlockSpec((1, d), lambda i, j: (0, 0)),
            pl.BlockSpec((d, tn), lambda i, j: (0, j)),
        ],
        out_specs=pl.BlockSpec((tm, tn), lambda i, j: (i, j)),
        scratch_shapes=[pltpu.VMEM((tm, d), BF16)],
        compiler_params=_params(("parallel", "arbitrary")),
        name=name,
    )(x, g.reshape(1, d), w)


def _matmul_residual_kernel(a_ref, w_ref, x_ref, o_ref):
    o_ref[...] = x_ref[...] + jnp.dot(a_ref[...], w_ref[...], preferred_element_type=F32)


def matmul_residual(a, w, x, *, tm=1024, tn=512, name):
    m, k = a.shape
    n = w.shape[1]
    tm, tn = _fit(m, tm), _fit(n, tn)
    return pl.pallas_call(
        _matmul_residual_kernel,
        out_shape=jax.ShapeDtypeStruct((m, n), F32),
        grid=(m // tm, n // tn),
        in_specs=[
            pl.BlockSpec((tm, k), lambda i, j: (i, 0)),
            pl.BlockSpec((k, tn), lambda i, j: (0, j)),
            pl.BlockSpec((tm, tn), lambda i, j: (i, j)),
        ],
        out_specs=pl.BlockSpec((tm, tn), lambda i, j: (i, j)),
        compiler_params=_params(("parallel", "arbitrary")),
        name=name,
    )(a, w, x)


def _mlp_kernel(x_ref, g_ref, w1_ref, w2_ref, gf_ref, o_ref, hn_ref, *, final_norm):
    f = pl.program_id(1)

    @pl.when(f == 0)
    def _():
        _norm_block_to(x_ref, g_ref, hn_ref)
        o_ref[...] = x_ref[...]

    a = jnp.dot(hn_ref[...], w1_ref[...], preferred_element_type=F32)
    a = jnp.maximum(a, 0.0)
    a = (a * a).astype(BF16)
    o_ref[...] += jnp.dot(a, w2_ref[...], preferred_element_type=F32)

    if final_norm:
        @pl.when(f == pl.num_programs(1) - 1)
        def _():
            _norm_block_to(o_ref, gf_ref, o_ref)


def mlp_residual(x, g, w1, w2, g_final, *, final_norm, tm=512, tf=1024, name):
    m, d = x.shape
    ff = w1.shape[1]
    tm, tf = _fit(m, tm), _fit(ff, tf)
    return pl.pallas_call(
        partial(_mlp_kernel, final_norm=final_norm),
        out_shape=jax.ShapeDtypeStruct((m, d), F32),
        grid=(m // tm, ff // tf),
        in_specs=[
            pl.BlockSpec((tm, d), lambda i, f: (i, 0)),
            pl.BlockSpec((1, d), lambda i, f: (0, 0)),
            pl.BlockSpec((d, tf), lambda i, f: (0, f)),
            pl.BlockSpec((tf, d), lambda i, f: (f, 0)),
            pl.BlockSpec((1, d), lambda i, f: (0, 0)),
        ],
        out_specs=pl.BlockSpec((tm, d), lambda i, f: (i, 0)),
        scratch_shapes=[pltpu.VMEM((tm, d), BF16)],
        compiler_params=_params(("parallel", "arbitrary")),
        name=name,
    )(x, g.reshape(1, d), w1, w2, g_final.reshape(1, d))


def _attn_kernel(lam_ref, g_ref, q_ref, k_ref, v_ref, o_ref, *, tq, tk, lambda_init):
    qi = pl.program_id(2)
    hd = DA_HEAD_DIM
    c = hd ** -0.5 * LOG2E

    lp = lam_ref[...]
    lam = (jnp.exp(jnp.sum(lp[0:1] * lp[1:2], axis=-1, keepdims=True))
           - jnp.exp(jnp.sum(lp[2:3] * lp[3:4], axis=-1, keepdims=True)) + lambda_init)

    q = q_ref[...]
    qs = (q[:, :hd], q[:, hd:])

    def kv_step(start, state, mask):
        k = k_ref[pl.ds(start, tk), :]
        v = v_ref[pl.ds(start, tk), :]
        new = []
        for t in range(2):
            m, l, acc = state[t]
            s = lax.dot_general(qs[t], k[:, t * hd:(t + 1) * hd], (((1,), (1,)), ((), ())),
                                preferred_element_type=F32)
            if mask is not None:
                s = jnp.where(mask, s, NEG_BIG)
            m_new = jnp.maximum(m, jnp.max(s, axis=-1, keepdims=True))
            alpha = jnp.exp2((m - m_new) * c)
            p = jnp.exp2((s - m_new) * c)
            l = alpha * l + jnp.sum(p, axis=-1, keepdims=True)
            acc = alpha * acc + jnp.dot(p.astype(BF16), v, preferred_element_type=F32)
            new.append((m_new, l, acc))
        return tuple(new)

    init = tuple((jnp.full((tq, 1), NEG_BIG, F32), jnp.zeros((tq, 1), F32),
                  jnp.zeros((tq, DA_VALUE_DIM), F32)) for _ in range(2))

    n_full = qi * (tq // tk)
    state = lax.fori_loop(
        0, n_full, lambda j, st: kv_step(pl.multiple_of(j * tk, tk), st, None), init)

    q_chunk = lax.broadcasted_iota(jnp.int32, (tq, tk), 0) // CHUNK
    for dblk in range(tq // tk):
        k_chunk = (lax.broadcasted_iota(jnp.int32, (tq, tk), 1) + dblk * tk) // CHUNK
        state = kv_step(pl.multiple_of(qi * tq + dblk * tk, tk), state, k_chunk <= q_chunk)

    (_, l0, a0), (_, l1, a1) = state
    o = a0 / l0 - lam * (a1 / l1)
    o = o * lax.rsqrt(jnp.mean(o * o, axis=-1, keepdims=True) + EPS)
    o = o * g_ref[...] * (1.0 - lambda_init)
    o_ref[...] = o.astype(o_ref.dtype)


def diff_attention_core(qkv, lam_p, subln_g, *, batch, seq, lambda_init, tq=256, tk=256, name):
    m, d3 = qkv.shape
    d = d3 // 3
    heads = d // DA_VALUE_DIM
    tq = _fit(seq, tq)
    tk = _fit(tq, tk)
    nq = seq // tq
    return pl.pallas_call(
        partial(_attn_kernel, tq=tq, tk=tk, lambda_init=lambda_init),
        out_shape=jax.ShapeDtypeStruct((m, d), BF16),
        grid=(batch, heads, nq),
        in_specs=[
            pl.BlockSpec((4, DA_HEAD_DIM), lambda b, h, i: (0, 0)),
            pl.BlockSpec((1, DA_VALUE_DIM), lambda b, h, i: (0, 0)),
            pl.BlockSpec((tq, DA_VALUE_DIM), lambda b, h, i: (b * nq + i, h)),
            pl.BlockSpec((seq, DA_VALUE_DIM), lambda b, h, i: (b, heads + h)),
            pl.BlockSpec((seq, DA_VALUE_DIM), lambda b, h, i: (b, 2 * heads + h)),
        ],
        out_specs=pl.BlockSpec((tq, DA_VALUE_DIM), lambda b, h, i: (b * nq + i, h)),
        compiler_params=_params(("parallel", "parallel", "arbitrary")),
        name=name,
    )(lam_p, subln_g.reshape(1, DA_VALUE_DIM), qkv, qkv, qkv)


def _retention_kernel(lg_ref, cos_ref, sin_ref, q_ref, k_ref, v_ref, g_ref, o_ref, r_ref, *, lc):
    h = pl.program_id(1)
    half = RET_QK_DIM // 2

    @pl.when(pl.program_id(2) == 0)
    def _():
        r_ref[...] = jnp.zeros_like(r_ref)

    lg = lg_ref[h]
    cos = cos_ref[...]
    sin = sin_ref[...]

    def rope(t):
        t = t.astype(F32)
        t1, t2 = t[:, :half], t[:, half:]
        return jnp.concatenate([t1 * cos - t2 * sin, t1 * sin + t2 * cos], axis=-1)

    qr = rope(q_ref[...])
    kr = rope(k_ref[...]) * RET_QK_DIM ** -0.5
    v = v_ref[...]

    pos = lax.broadcasted_iota(jnp.int32, (lc, 1), 0).astype(F32)
    q_decay = jnp.exp(lg * (pos + 1.0))
    k_decay = jnp.exp(lg * (lc - 1.0 - pos))
    ni = lax.broadcasted_iota(jnp.int32, (lc, lc), 0)
    mi = lax.broadcasted_iota(jnp.int32, (lc, lc), 1)
    decay = jnp.where(mi // CHUNK <= ni // CHUNK,
                      jnp.exp(lg * jnp.abs(ni - mi).astype(F32)), 0.0)

    s = lax.dot_general(qr.astype(BF16), kr.astype(BF16), (((1,), (1,)), ((), ())),
                        preferred_element_type=F32) * decay
    r = r_ref[...]
    o = jnp.dot(s.astype(BF16), v, preferred_element_type=F32)
    o = o + jnp.dot((qr * q_decay).astype(BF16), r.astype(BF16), preferred_element_type=F32)
    r_ref[...] = r * jnp.exp(lg * lc) + lax.dot_general(
        (kr * k_decay).astype(BF16), v, (((0,), (0,)), ((), ())), preferred_element_type=F32)

    o = o * lax.rsqrt(jnp.mean(o * o, axis=-1, keepdims=True) + EPS)
    gate = g_ref[...].astype(F32)
    o_ref[...] = (gate * jax.nn.sigmoid(gate) * o).astype(o_ref.dtype)


def retention_core(proj, *, batch, seq, lc=256, name):
    m, d6 = proj.shape
    d = d6 // 6
    heads = d // RET_QK_DIM
    lc = _fit(seq, lc)
    nc = seq // lc
    half = RET_QK_DIM // 2
    log_gamma = jnp.log(1.0 - jnp.exp2(-5.0 - jnp.arange(heads, dtype=F32)))
    inv = 1.0 / (ROPE_BASE ** jnp.linspace(0.0, 1.0, half, dtype=F32))
    ang = jnp.arange(seq, dtype=F32)[:, None] * inv[None, :]
    cos, sin = jnp.cos(ang), jnp.sin(ang)
    vmem = pltpu.VMEM
    return pl.pallas_call(
        partial(_retention_kernel, lc=lc),
        out_shape=jax.ShapeDtypeStruct((m, 2 * d), BF16),
        grid=(batch, heads, nc),
        in_specs=[
            pl.BlockSpec(memory_space=pltpu.SMEM),
            pl.BlockSpec((lc, half), lambda b, h, c: (c, 0)),
            pl.BlockSpec((lc, half), lambda b, h, c: (c, 0)),
            pl.BlockSpec((lc, RET_QK_DIM), lambda b, h, c: (b * nc + c, h)),
            pl.BlockSpec((lc, RET_QK_DIM), lambda b, h, c: (b * nc + c, heads + h)),
            pl.BlockSpec((lc, RET_V_DIM), lambda b, h, c: (b * nc + c, heads + h)),
            pl.BlockSpec((lc, RET_V_DIM), lambda b, h, c: (b * nc + c, 2 * heads + h)),
        ],
        out_specs=pl.BlockSpec((lc, RET_V_DIM), lambda b, h, c: (b * nc + c, h)),
        scratch_shapes=[vmem((RET_QK_DIM, RET_V_DIM), F32)],
        compiler_params=_params(("parallel", "parallel", "arbitrary")),
        name=name,
    )(log_gamma, cos, sin, proj, proj, proj, proj)


def _norm_only_kernel(x_ref, g_ref, o_ref):
    _norm_block_to(x_ref, g_ref, o_ref)


def rmsnorm_bf16(x, g, *, tm=1024, name):
    m, d = x.shape
    tm = _fit(m, tm)
    return pl.pallas_call(
        _norm_only_kernel,
        out_shape=jax.ShapeDtypeStruct((m, d), BF16),
        grid=(m // tm,),
        in_specs=[pl.BlockSpec((tm, d), lambda i: (i, 0)), pl.BlockSpec((1, d), lambda i: (0, 0))],
        out_specs=pl.BlockSpec((tm, d), lambda i: (i, 0)),
        compiler_params=_params(("parallel",)),
        name=name,
    )(x, g.reshape(1, d))


def _s5_kernel(u_ref, m_ref, wz_ref, wy_ref, a_ref, y_ref, z_ref, *, row_chunk):
    rows = u_ref.shape[1]
    ns = z_ref.shape[1] // 2
    n_sub = rows // S5_SEQS
    n_chunks = rows // row_chunk

    def z_body(i, carry):
        rs = pl.ds(pl.multiple_of(i * row_chunk, row_chunk), row_chunk)
        z_ref[rs, :] = jnp.dot(u_ref[0, rs, :], wz_ref[0], preferred_element_type=F32)
        return carry

    lax.fori_loop(0, n_chunks, z_body, 0)

    a_re = jnp.broadcast_to(a_ref[0, 0:1, :], (S5_SEQS, ns))
    a_im = jnp.broadcast_to(a_ref[0, 1:2, :], (S5_SEQS, ns))

    def advance(k, sr, si):
        rs = pl.ds(pl.multiple_of(k * S5_SEQS, S5_SEQS), S5_SEQS)
        zr = z_ref[rs, :ns]
        zi = z_ref[rs, ns:]
        return rs, a_re * sr - a_im * si + zr, a_re * si + a_im * sr + zi

    def pass1(k, st):
        _, nr, ni = advance(k, *st)
        return nr, ni

    zero = jnp.zeros((S5_SEQS, ns), F32)
    end_re, end_im = lax.fori_loop(0, n_sub, pass1, (zero, zero))

    odd = lax.broadcasted_iota(jnp.int32, (S5_SEQS, ns), 0) % 2 == 1
    init = (jnp.where(odd, pltpu.roll(end_re, 1, axis=0), 0.0),
            jnp.where(odd, pltpu.roll(end_im, 1, axis=0), 0.0))

    def pass2(k, st):
        rs, nr, ni = advance(k, *st)
        z_ref[rs, :ns] = st[0]
        z_ref[rs, ns:] = st[1]
        return nr, ni

    lax.fori_loop(0, n_sub, pass2, init)

    def y_body(i, carry):
        rs = pl.ds(pl.multiple_of(i * row_chunk, row_chunk), row_chunk)
        y_ref[0, rs, :] = (jnp.dot(u_ref[0, rs, :], m_ref[0], preferred_element_type=F32)
                           + jnp.dot(z_ref[rs, :].astype(BF16), wy_ref[0], preferred_element_type=F32))
        return carry

    lax.fori_loop(0, n_chunks, y_body, 0)


def s5_core(ucat, m_op, wz, wy, a_sub, *, row_chunk=512, name):
    tiles, rows, lk = ucat.shape
    ns2 = wz.shape[2]
    row_chunk = _fit(rows, row_chunk)
    return pl.pallas_call(
        partial(_s5_kernel, row_chunk=row_chunk),
        out_shape=jax.ShapeDtypeStruct((tiles, rows, lk), F32),
        grid=(tiles,),
        in_specs=[
            pl.BlockSpec((1, rows, lk), lambda j: (j, 0, 0)),
            pl.BlockSpec((1, lk, lk), lambda j: (j, 0, 0)),
            pl.BlockSpec((1, lk, ns2), lambda j: (j, 0, 0)),
            pl.BlockSpec((1, ns2, lk), lambda j: (j, 0, 0)),
            pl.BlockSpec((1, 2, ns2 // 2), lambda j: (j, 0, 0)),
        ],
        out_specs=pl.BlockSpec((1, rows, lk), lambda j: (j, 0, 0)),
        scratch_shapes=[pltpu.VMEM((rows, ns2), F32)],
        compiler_params=_params(("parallel",)),
        name=name,
    )(ucat, m_op, wz, wy, a_sub)


def _s5_operators(a_re, a_im, log_dt, b_re, b_im, c_re, c_im):
    g, p = a_re.shape
    c = S5_GROUP
    gpt = LANES // c
    tiles = g // gpt
    ls = S5_SUB

    lam_re, lam_im = a_re.astype(F32), a_im.astype(F32)
    dt = jnp.exp(log_dt.astype(F32))[:, None]
    mag = jnp.exp(lam_re * dt)
    ab_re = mag * jnp.cos(lam_im * dt)
    ab_im = mag * jnp.sin(lam_im * dt)
    den = lam_re * lam_re + lam_im * lam_im
    nr, ni = ab_re - 1.0, ab_im
    coef_re = (nr * lam_re + ni * lam_im) / den
    coef_im = (ni * lam_re - nr * lam_im) / den
    br, bi = b_re.astype(F32), b_im.astype(F32)
    bb_re = coef_re[..., None] * br - coef_im[..., None] * bi
    bb_im = coef_re[..., None] * bi + coef_im[..., None] * br
    cr, ci = c_re.astype(F32), c_im.astype(F32)

    pw_re, pw_im = [jnp.ones_like(ab_re)], [jnp.zeros_like(ab_im)]
    for _ in range(ls):
        r0, i0 = pw_re[-1], pw_im[-1]
        pw_re.append(r0 * ab_re - i0 * ab_im)
        pw_im.append(r0 * ab_im + i0 * ab_re)
    pw_re, pw_im = jnp.stack(pw_re), jnp.stack(pw_im)

    eye = jnp.eye(gpt, dtype=F32)

    def block_diag(t, rows_per_group, cols_per_group):
        lead = t.shape[:-3]
        t = t.reshape(*lead, tiles, gpt, rows_per_group, cols_per_group)
        t = jnp.einsum('...tgrc,gh->...tgrhc', t, eye)
        return t.reshape(*lead, tiles, gpt * rows_per_group, gpt * cols_per_group)

    cb_re = jnp.einsum('gop,tgp->tgop', cr, pw_re[:ls]) - jnp.einsum('gop,tgp->tgop', ci, pw_im[:ls])
    cb_im = jnp.einsum('gop,tgp->tgop', cr, pw_im[:ls]) + jnp.einsum('gop,tgp->tgop', ci, pw_re[:ls])
    kern = (jnp.einsum('tgop,gpi->tgio', cb_re, bb_re, precision='highest')
            - jnp.einsum('tgop,gpi->tgio', cb_im, bb_im, precision='highest'))
    kern = block_diag(kern, c, c)
    zero_blk = jnp.zeros_like(kern[0])
    m_rows = []
    for s in range(ls):
        m_rows.append(jnp.concatenate([kern[t - s] if t >= s else zero_blk for t in range(ls)], axis=-1))
    m_op = jnp.concatenate(m_rows, axis=-2)

    rev_re, rev_im = pw_re[ls - 1::-1], pw_im[ls - 1::-1]
    wz_re = jnp.einsum('sgp,gpi->sgip', rev_re, bb_re) - jnp.einsum('sgp,gpi->sgip', rev_im, bb_im)
    wz_im = jnp.einsum('sgp,gpi->sgip', rev_re, bb_im) + jnp.einsum('sgp,gpi->sgip', rev_im, bb_re)
    wz_re, wz_im = block_diag(wz_re, c, p), block_diag(wz_im, c, p)
    wz = jnp.concatenate([wz_re, wz_im], axis=-1)
    wz = wz.transpose(1, 0, 2, 3).reshape(tiles, ls * LANES, 2 * gpt * p)

    wy_re = jnp.einsum('gop,tgp->tgpo', cr, pw_re[1:]) - jnp.einsum('gop,tgp->tgpo', ci, pw_im[1:])
    wy_im = -(jnp.einsum('gop,tgp->tgpo', cr, pw_im[1:]) + jnp.einsum('gop,tgp->tgpo', ci, pw_re[1:]))
    wy_re, wy_im = block_diag(wy_re, p, c), block_diag(wy_im, p, c)
    wy = jnp.concatenate([wy_re, wy_im], axis=-2)
    wy = wy.transpose(1, 2, 0, 3).reshape(tiles, 2 * gpt * p, ls * LANES)

    a_sub = jnp.stack([pw_re[ls].reshape(tiles, gpt * p), pw_im[ls].reshape(tiles, gpt * p)], axis=1)
    return m_op.astype(BF16), wz.astype(BF16), wy.astype(BF16), a_sub


def _s5_glu_kernel(xf_ref, y_ref, g_ref, d_ref, wv_ref, wg_ref, xb_ref, o_ref, act_ref):
    @pl.when(pl.program_id(1) == 0)
    def _():
        rows_total = xf_ref.shape[0]
        step = min(NORM_ROWS, rows_total)
        g = g_ref[...]
        dsk = d_ref[...]

        def body(r, carry):
            rows = pl.ds(pl.multiple_of(r * step, step), step)
            hn = _rmsnorm(xf_ref[rows, :], g)
            act_ref[rows, :] = jax.nn.gelu(y_ref[rows, :] + dsk * hn).astype(act_ref.dtype)
            return carry

        lax.fori_loop(0, rows_total // step, body, 0)

    act = act_ref[...]
    val = jnp.dot(act, wv_ref[...], preferred_element_type=F32)
    gate = jnp.dot(act, wg_ref[...], preferred_element_type=F32)
    o_ref[...] = xb_ref[...] + val * jax.nn.sigmoid(gate)


def s5_glu_residual(x, y, g, d_skip, w_glu, *, tm=512, tn=512, name):
    m, d = x.shape
    tm, tn = _fit(m, tm), _fit(d, tn)
    nb = d // tn
    return pl.pallas_call(
        _s5_glu_kernel,
        out_shape=jax.ShapeDtypeStruct((m, d), F32),
        grid=(m // tm, nb),
        in_specs=[
            pl.BlockSpec((tm, d), lambda i, j: (i, 0)),
            pl.BlockSpec((tm, d), lambda i, j: (i, 0)),
            pl.BlockSpec((1, d), lambda i, j: (0, 0)),
            pl.BlockSpec((1, d), lambda i, j: (0, 0)),
            pl.BlockSpec((d, tn), lambda i, j: (0, j)),
            pl.BlockSpec((d, tn), lambda i, j: (0, nb + j)),
            pl.BlockSpec((tm, tn), lambda i, j: (i, j)),
        ],
        out_specs=pl.BlockSpec((tm, tn), lambda i, j: (i, j)),
        scratch_shapes=[pltpu.VMEM((tm, d), BF16)],
        compiler_params=_params(("parallel", "arbitrary")),
        name=name,
    )(x, y, g.reshape(1, d), d_skip.reshape(1, d), w_glu, w_glu, x)


def s5_mixer_residual(x, g, a_re, a_im, log_dt, b_re, b_im, c_re, c_im, d_skip, w_glu, *, batch, seq, tag):
    m, d = x.shape
    tiles = d // LANES
    n_sub = seq // (2 * S5_SUB)
    h = rmsnorm_bf16(x, g, name=f"s5_norm_{tag}")
    u = h.reshape(batch, 2, n_sub, S5_SUB, tiles, LANES).transpose(4, 2, 0, 1, 3, 5)
    u = u.reshape(tiles, n_sub * S5_SEQS, S5_SUB * LANES)
    m_op, wz, wy, a_sub = _s5_operators(a_re, a_im, log_dt, b_re, b_im, c_re, c_im)
    y = s5_core(u, m_op, wz, wy, a_sub, name=f"s5_scan_{tag}")
    y = y.reshape(tiles, n_sub, batch, 2, S5_SUB, LANES).transpose(2, 3, 1, 4, 0, 5).reshape(m, d)
    return s5_glu_residual(x, y, g, d_skip, w_glu, name=f"s5_glu_{tag}")


def kernel(x, norm_mix, norm_mlp, norm_final, a_w_in, a_lambda, a_subln, a_w_out, b_a_re, b_a_im, b_log_dt, b_b_re, b_b_im, b_c_re, b_c_im, b_d, b_w_glu, c_w_in, c_w_out, mlp_w1, mlp_w2):
    batch, seq, d = x.shape
    depth = norm_mix.shape[0]
    assert batch * 2 == S5_SEQS, "S5 scan layout places batch x two time halves on the 8 sublanes"
    xf = x.reshape(batch * seq, d)

    for i in range(depth):
        kind = i % N_MIXERS
        j = i // N_MIXERS
        if kind == 0:
            lambda_init = 0.8 - 0.6 * math.exp(-0.3 * i)
            qkv = norm_matmul(xf, norm_mix[i], a_w_in[j].astype(BF16), name=f"attn_in_{i}")
            o = diff_attention_core(qkv, a_lambda[j], a_subln[j], batch=batch, seq=seq,
                                    lambda_init=lambda_init, name=f"attn_core_{i}")
            xf = matmul_residual(o, a_w_out[j].astype(BF16), xf, name=f"attn_out_{i}")
        elif kind == 1:
            xf = s5_mixer_residual(xf, norm_mix[i], b_a_re[j], b_a_im[j], b_log_dt[j], b_b_re[j],
                                   b_b_im[j], b_c_re[j], b_c_im[j], b_d[j], b_w_glu[j].astype(BF16),
                                   batch=batch, seq=seq, tag=str(i))
        else:
            proj = norm_matmul(xf, norm_mix[i], c_w_in[j].astype(BF16), name=f"ret_in_{i}")
            o = retention_core(proj, batch=batch, seq=seq, name=f"ret_core_{i}")
            xf = matmul_residual(o, c_w_out[j].astype(BF16), xf, name=f"ret_out_{i}")
        xf = mlp_residual(xf, norm_mlp[i], mlp_w1[i].astype(BF16), mlp_w2[i].astype(BF16), norm_final,
                          final_norm=(i == depth - 1), name=f"mlp_{i}")
    return xf.reshape(batch, seq, d)
```

```python
import math
from functools import partial

import jax
import jax.numpy as jnp
from jax import lax
from jax.experimental import pallas as pl
from jax.experimental.pallas import tpu as pltpu

F32 = jnp.float32
BF16 = jnp.bfloat16

EPS = 1e-6
CHUNK = 64
N_MIXERS = 3
DA_HEAD_DIM = 128
DA_VALUE_DIM = 2 * DA_HEAD_DIM
RET_QK_DIM = 256
RET_V_DIM = 2 * RET_QK_DIM
ROPE_BASE = 10000.0
S5_GROUP = 16
S5_STATE = 64

LANES = 128
SUBLANES = 8
VMEM_LIMIT_BYTES = 56 * 1024 * 1024
LOG2E = math.log2(math.e)
NEG_BIG = -1e30

NORM_ROWS = 256
S5_SUB = 8
S5_SEQS = SUBLANES


def _params(semantics):
    return pltpu.CompilerParams(dimension_semantics=semantics, vmem_limit_bytes=VMEM_LIMIT_BYTES)


def _fit(n, preferred):
    t = min(preferred, n)
    while n % t:
        t //= 2
    return t


def _rmsnorm(x, g):
    y = x * lax.rsqrt(jnp.mean(x * x, axis=-1, keepdims=True) + EPS)
    return y * g


def _norm_block_to(x_ref, g_ref, dst_ref):
    rows_total = x_ref.shape[0]
    step = min(NORM_ROWS, rows_total)
    g = g_ref[...]

    def body(r, carry):
        rows = pl.ds(pl.multiple_of(r * step, step), step)
        dst_ref[rows, :] = _rmsnorm(x_ref[rows, :], g).astype(dst_ref.dtype)
        return carry

    lax.fori_loop(0, rows_total // step, body, 0)


def _norm_matmul_kernel(x_ref, g_ref, w_ref, o_ref, hn_ref, *, scaled_blocks, scale):
    j = pl.program_id(1)

    @pl.when(j == 0)
    def _():
        _norm_block_to(x_ref, g_ref, hn_ref)

    acc = jnp.dot(hn_ref[...], w_ref[...], preferred_element_type=F32)
    if scaled_blocks:
        acc = acc * jnp.where(j < scaled_blocks, scale, 1.0)
    o_ref[...] = acc.astype(o_ref.dtype)


def norm_matmul(x, g, w, *, scaled_cols=0, scale=1.0, tm=1024, tn=1024, name):
    m, d = x.shape
    n = w.shape[1]
    tm, tn = _fit(m, tm), _fit(n, tn)
    assert scaled_cols % tn == 0
    return pl.pallas_call(
        partial(_norm_matmul_kernel, scaled_blocks=scaled_cols // tn, scale=scale),
        out_shape=jax.ShapeDtypeStruct((m, n), BF16),
        grid=(m // tm, n // tn),
        in_specs=[
            pl.BlockSpec((tm, d), lambda i, j: (i, 0)),
            pl.BlockSpec((1, d), lambda i, j: (0, 0)),
            pl.BlockSpec((d, tn), lambda i, j: (0, j)),
        ],
        out_specs=pl.BlockSpec((tm, tn), lambda i, j: (i, j)),
        scratch_shapes=[pltpu.VMEM((tm, d), BF16)],
        compiler_params=_params(("parallel", "arbitrary")),
        name=name,
    )(x, g.reshape(1, d), w)


def _matmul_residual_kernel(a_ref, w_ref, x_ref, o_ref):
    o_ref[...] = x_ref[...] + jnp.dot(a_ref[...], w_ref[...], preferred_element_type=F32)


def matmul_residual(a, w, x, *, tm=1024, tn=512, name):
    m, k = a.shape
    n = w.shape[1]
    tm, tn = _fit(m, tm), _fit(n, tn)
    return pl.pallas_call(
        _matmul_residual_kernel,
        out_shape=jax.ShapeDtypeStruct((m, n), F32),
        grid=(m // tm, n // tn),
        in_specs=[
            pl.BlockSpec((tm, k), lambda i, j: (i, 0)),
            pl.BlockSpec((k, tn), lambda i, j: (0, j)),
            pl.BlockSpec((tm, tn), lambda i, j: (i, j)),
        ],
        out_specs=pl.BlockSpec((tm, tn), lambda i, j: (i, j)),
        compiler_params=_params(("parallel", "arbitrary")),
        name=name,
    )(a, w, x)


def _mlp_kernel(x_ref, g_ref, w1_ref, w2_ref, gf_ref, o_ref, hn_ref, *, final_norm):
    f = pl.program_id(1)

    @pl.when(f == 0)
    def _():
        _norm_block_to(x_ref, g_ref, hn_ref)
        o_ref[...] = x_ref[...]

    a = jnp.dot(hn_ref[...], w1_ref[...], preferred_element_type=F32)
    a = jnp.maximum(a, 0.0)
    a = (a * a).astype(BF16)
    o_ref[...] += jnp.dot(a, w2_ref[...], preferred_element_type=F32)

    if final_norm:
        @pl.when(f == pl.num_programs(1) - 1)
        def _():
            _norm_block_to(o_ref, gf_ref, o_ref)


def mlp_residual(x, g, w1, w2, g_final, *, final_norm, tm=512, tf=1024, name):
    m, d = x.shape
    ff = w1.shape[1]
    tm, tf = _fit(m, tm), _fit(ff, tf)
    return pl.pallas_call(
        partial(_mlp_kernel, final_norm=final_norm),
        out_shape=jax.ShapeDtypeStruct((m, d), F32),
        grid=(m // tm, ff // tf),
        in_specs=[
            pl.BlockSpec((tm, d), lambda i, f: (i, 0)),
            pl.BlockSpec((1, d), lambda i, f: (0, 0)),
            pl.BlockSpec((d, tf), lambda i, f: (0, f)),
            pl.BlockSpec((tf, d), lambda i, f: (f, 0)),
            pl.BlockSpec((1, d), lambda i, f: (0, 0)),
        ],
        out_specs=pl.BlockSpec((tm, d), lambda i, f: (i, 0)),
        scratch_shapes=[pltpu.VMEM((tm, d), BF16)],
        compiler_params=_params(("parallel", "arbitrary")),
        name=name,
    )(x, g.reshape(1, d), w1, w2, g_final.reshape(1, d))


def _attn_kernel(lam_ref, g_ref, q_ref, k_ref, v_ref, o_ref, qs_ref, m_ref, l_ref, alpha_ref, acc_ref, p_ref,
                 sa_ref, sb_ref, *, tq, tk, row_chunk, lambda_init):
    qi = pl.program_id(2)
    hd = DA_HEAD_DIM
    lane_reps = tk // LANES

    zeros = jnp.zeros((tq, hd), qs_ref.dtype)
    qs_ref[0:tq, 0:hd] = q_ref[:, 0:hd]
    qs_ref[0:tq, hd:] = zeros
    qs_ref[tq:, 0:hd] = zeros
    qs_ref[tq:, hd:] = q_ref[:, hd:]

    m_ref[...] = jnp.full(m_ref.shape, NEG_BIG, F32)
    l_ref[...] = jnp.zeros(l_ref.shape, F32)
    acc_ref[...] = jnp.zeros(acc_ref.shape, F32)

    def scores_to(s_ref, blk):
        start = pl.multiple_of(blk * tk, tk)
        s_ref[...] = lax.dot_general(qs_ref[...], k_ref[pl.ds(start, tk), :], (((1,), (1,)), ((), ())),
                                     preferred_element_type=F32)

    def absorb(s_ref, blk, masked):
        start = pl.multiple_of(blk * tk, tk)
        for r0 in range(0, 2 * tq, row_chunk):
            rows = slice(r0, r0 + row_chunk)
            s = s_ref[rows, :]
            if masked:
                q_pos = lax.broadcasted_iota(jnp.int32, (row_chunk, tk), 0) + (r0 % tq + qi * tq)
                k_pos = lax.broadcasted_iota(jnp.int32, (row_chunk, tk), 1) + start
                s = jnp.where(k_pos // CHUNK <= q_pos // CHUNK, s, NEG_BIG)
            m_prev = m_ref[rows, :]
            m_new = jnp.maximum(m_prev, jnp.max(s, axis=-1, keepdims=True))
            alpha = jnp.exp2(m_prev - m_new)
            p = jnp.exp2(s - jnp.tile(m_new, (1, lane_reps)))
            l_ref[rows, :] = alpha * l_ref[rows, :] + jnp.sum(p, axis=-1, keepdims=True)
            p_ref[rows, :] = p.astype(BF16)
            alpha_ref[rows, :] = alpha
            m_ref[rows, :] = m_new
        pv = jnp.dot(p_ref[...], v_ref[pl.ds(start, tk), :], preferred_element_type=F32)
        acc_ref[...] = jnp.tile(alpha_ref[...], (1, DA_VALUE_DIM // LANES)) * acc_ref[...] + pv

    n_full = (qi * tq) // tk
    n_pairs = n_full // 2
    scores_to(sa_ref, 0)

    def pair_step(jj, carry):
        j = 2 * jj
        absorb(sa_ref, j, False)
        scores_to(sb_ref, j + 1)
        absorb(sb_ref, j + 1, False)
        scores_to(sa_ref, j + 2)
        return carry

    lax.fori_loop(0, n_pairs, pair_step, 0)

    @pl.when(n_full % 2 == 0)
    def _():
        absorb(sa_ref, n_full, True)

    @pl.when(n_full % 2 == 1)
    def _():
        absorb(sa_ref, n_full - 1, False)
        scores_to(sb_ref, n_full)
        absorb(sb_ref, n_full, True)

    lp = lam_ref[...]
    lam = (jnp.exp(jnp.sum(lp[0:1] * lp[1:2], axis=-1, keepdims=True))
           - jnp.exp(jnp.sum(lp[2:3] * lp[3:4], axis=-1, keepdims=True)) + lambda_init)
    inv_l = jnp.tile(1.0 / l_ref[...], (1, DA_VALUE_DIM // LANES))
    o = acc_ref[...] * inv_l
    o = o[:tq] - lam * o[tq:]
    o = o * lax.rsqrt(jnp.mean(o * o, axis=-1, keepdims=True) + EPS)
    o = o * g_ref[...] * (1.0 - lambda_init)
    o_ref[...] = o.astype(o_ref.dtype)


ATTN_Q_SCALE = DA_HEAD_DIM ** -0.5 * LOG2E


def diff_attention_core(qkv, lam_p, subln_g, *, batch, seq, lambda_init, tq=512, tk=512, row_chunk=64, name):
    m, d3 = qkv.shape
    d = d3 // 3
    heads = d // DA_VALUE_DIM
    tq = _fit(seq, tq)
    tk = _fit(seq, max(tk, tq))
    assert tk % tq == 0 and tq % row_chunk == 0
    nq = seq // tq
    return pl.pallas_call(
        partial(_attn_kernel, tq=tq, tk=tk, row_chunk=row_chunk, lambda_init=lambda_init),
        out_shape=jax.ShapeDtypeStruct((m, d), BF16),
        grid=(batch, heads, nq),
        in_specs=[
            pl.BlockSpec((4, DA_HEAD_DIM), lambda b, h, i: (0, 0)),
            pl.BlockSpec((1, DA_VALUE_DIM), lambda b, h, i: (0, 0)),
            pl.BlockSpec((tq, DA_VALUE_DIM), lambda b, h, i: (b * nq + i, h)),
            pl.BlockSpec((seq, DA_VALUE_DIM), lambda b, h, i: (b, heads + h)),
            pl.BlockSpec((seq, DA_VALUE_DIM), lambda b, h, i: (b, 2 * heads + h)),
        ],
        out_specs=pl.BlockSpec((tq, DA_VALUE_DIM), lambda b, h, i: (b * nq + i, h)),
        scratch_shapes=[pltpu.VMEM((2 * tq, DA_VALUE_DIM), BF16), pltpu.VMEM((2 * tq, LANES), F32),
                        pltpu.VMEM((2 * tq, LANES), F32), pltpu.VMEM((2 * tq, LANES), F32),
                        pltpu.VMEM((2 * tq, DA_VALUE_DIM), F32), pltpu.VMEM((2 * tq, tk), BF16),
                        pltpu.VMEM((2 * tq, tk), F32), pltpu.VMEM((2 * tq, tk), F32)],
        compiler_params=_params(("parallel", "parallel", "arbitrary")),
        name=name,
    )(lam_p, subln_g.reshape(1, DA_VALUE_DIM), qkv, qkv, qkv)


def _retention_kernel(lg_ref, cos_ref, sin_ref, q_ref, k_ref, v_ref, g_ref, o_ref, r_ref, *, lc):
    h = pl.program_id(1)
    half = RET_QK_DIM // 2

    @pl.when(pl.program_id(2) == 0)
    def _():
        r_ref[...] = jnp.zeros_like(r_ref)

    lg = lg_ref[h]
    cos = cos_ref[...]
    sin = sin_ref[...]

    def rope(t):
        t = t.astype(F32)
        t1, t2 = t[:, :half], t[:, half:]
        return jnp.concatenate([t1 * cos - t2 * sin, t1 * sin + t2 * cos], axis=-1)

    qr = rope(q_ref[...])
    kr = rope(k_ref[...]) * RET_QK_DIM ** -0.5
    v = v_ref[...]

    pos = lax.broadcasted_iota(jnp.int32, (lc, 1), 0).astype(F32)
    q_decay = jnp.exp(lg * (pos + 1.0))
    k_decay = jnp.exp(lg * (lc - 1.0 - pos))
    ni = lax.broadcasted_iota(jnp.int32, (lc, lc), 0)
    mi = lax.broadcasted_iota(jnp.int32, (lc, lc), 1)
    decay = jnp.where(mi // CHUNK <= ni // CHUNK,
                      jnp.exp(lg * jnp.abs(ni - mi).astype(F32)), 0.0)

    s = lax.dot_general(qr.astype(BF16), kr.astype(BF16), (((1,), (1,)), ((), ())),
                        preferred_element_type=F32) * decay
    r = r_ref[...]
    o = jnp.dot(s.astype(BF16), v, preferred_element_type=F32)
    o = o + jnp.dot((qr * q_decay).astype(BF16), r.astype(BF16), preferred_element_type=F32)
    r_ref[...] = r * jnp.exp(lg * lc) + lax.dot_general(
        (kr * k_decay).astype(BF16), v, (((0,), (0,)), ((), ())), preferred_element_type=F32)

    o = o * lax.rsqrt(jnp.mean(o * o, axis=-1, keepdims=True) + EPS)
    gate = g_ref[...].astype(F32)
    o_ref[...] = (gate * jax.nn.sigmoid(gate) * o).astype(o_ref.dtype)


def retention_core(proj, *, batch, seq, lc=256, name):
    m, d6 = proj.shape
    d = d6 // 6
    heads = d // RET_QK_DIM
    lc = _fit(seq, lc)
    nc = seq // lc
    half = RET_QK_DIM // 2
    log_gamma = jnp.log(1.0 - jnp.exp2(-5.0 - jnp.arange(heads, dtype=F32)))
    inv = 1.0 / (ROPE_BASE ** jnp.linspace(0.0, 1.0, half, dtype=F32))
    ang = jnp.arange(seq, dtype=F32)[:, None] * inv[None, :]
    cos, sin = jnp.cos(ang), jnp.sin(ang)
    vmem = pltpu.VMEM
    return pl.pallas_call(
        partial(_retention_kernel, lc=lc),
        out_shape=jax.ShapeDtypeStruct((m, 2 * d), BF16),
        grid=(batch, heads, nc),
        in_specs=[
            pl.BlockSpec(memory_space=pltpu.SMEM),
            pl.BlockSpec((lc, half), lambda b, h, c: (c, 0)),
            pl.BlockSpec((lc, half), lambda b, h, c: (c, 0)),
            pl.BlockSpec((lc, RET_QK_DIM), lambda b, h, c: (b * nc + c, h)),
            pl.BlockSpec((lc, RET_QK_DIM), lambda b, h, c: (b * nc + c, heads + h)),
            pl.BlockSpec((lc, RET_V_DIM), lambda b, h, c: (b * nc + c, heads + h)),
            pl.BlockSpec((lc, RET_V_DIM), lambda b, h, c: (b * nc + c, 2 * heads + h)),
        ],
        out_specs=pl.BlockSpec((lc, RET_V_DIM), lambda b, h, c: (b * nc + c, h)),
        scratch_shapes=[vmem((RET_QK_DIM, RET_V_DIM), F32)],
        compiler_params=_params(("parallel", "parallel", "arbitrary")),
        name=name,
    )(log_gamma, cos, sin, proj, proj, proj, proj)


def _norm_only_kernel(x_ref, g_ref, o_ref):
    _norm_block_to(x_ref, g_ref, o_ref)


def rmsnorm_bf16(x, g, *, tm=1024, name):
    m, d = x.shape
    tm = _fit(m, tm)
    return pl.pallas_call(
        _norm_only_kernel,
        out_shape=jax.ShapeDtypeStruct((m, d), BF16),
        grid=(m // tm,),
        in_specs=[pl.BlockSpec((tm, d), lambda i: (i, 0)), pl.BlockSpec((1, d), lambda i: (0, 0))],
        out_specs=pl.BlockSpec((tm, d), lambda i: (i, 0)),
        compiler_params=_params(("parallel",)),
        name=name,
    )(x, g.reshape(1, d))


def _s5_kernel(u_ref, m_ref, wz_ref, wy_ref, a_ref, y_ref, z_ref, *, row_chunk):
    rows = u_ref.shape[1]
    ns = z_ref.shape[1] // 2
    n_sub = rows // S5_SEQS
    n_chunks = rows // row_chunk

    def z_body(i, carry):
        rs = pl.ds(pl.multiple_of(i * row_chunk, row_chunk), row_chunk)
        z_ref[rs, :] = jnp.dot(u_ref[0, rs, :], wz_ref[0], preferred_element_type=F32)
        return carry

    lax.fori_loop(0, n_chunks, z_body, 0)

    a_re = jnp.broadcast_to(a_ref[0, 0:1, :], (S5_SEQS, ns))
    a_im = jnp.broadcast_to(a_ref[0, 1:2, :], (S5_SEQS, ns))

    def advance(k, sr, si):
        rs = pl.ds(pl.multiple_of(k * S5_SEQS, S5_SEQS), S5_SEQS)
        zr = z_ref[rs, :ns]
        zi = z_ref[rs, ns:]
        return rs, a_re * sr - a_im * si + zr, a_re * si + a_im * sr + zi

    def pass1(k, st):
        _, nr, ni = advance(k, *st)
        return nr, ni

    zero = jnp.zeros((S5_SEQS, ns), F32)
    end_re, end_im = lax.fori_loop(0, n_sub, pass1, (zero, zero))

    odd = lax.broadcasted_iota(jnp.int32, (S5_SEQS, ns), 0) % 2 == 1
    init = (jnp.where(odd, pltpu.roll(end_re, 1, axis=0), 0.0),
            jnp.where(odd, pltpu.roll(end_im, 1, axis=0), 0.0))

    def pass2(k, st):
        rs, nr, ni = advance(k, *st)
        z_ref[rs, :ns] = st[0]
        z_ref[rs, ns:] = st[1]
        return nr, ni

    lax.fori_loop(0, n_sub, pass2, init)

    def y_body(i, carry):
        rs = pl.ds(pl.multiple_of(i * row_chunk, row_chunk), row_chunk)
        y_ref[0, rs, :] = (jnp.dot(u_ref[0, rs, :], m_ref[0], preferred_element_type=F32)
                           + jnp.dot(z_ref[rs, :].astype(BF16), wy_ref[0], preferred_element_type=F32))
        return carry

    lax.fori_loop(0, n_chunks, y_body, 0)


def s5_core(ucat, m_op, wz, wy, a_sub, *, row_chunk=512, name):
    tiles, rows, lk = ucat.shape
    ns2 = wz.shape[2]
    row_chunk = _fit(rows, row_chunk)
    return pl.pallas_call(
        partial(_s5_kernel, row_chunk=row_chunk),
        out_shape=jax.ShapeDtypeStruct((tiles, rows, lk), F32),
        grid=(tiles,),
        in_specs=[
            pl.BlockSpec((1, rows, lk), lambda j: (j, 0, 0)),
            pl.BlockSpec((1, lk, lk), lambda j: (j, 0, 0)),
            pl.BlockSpec((1, lk, ns2), lambda j: (j, 0, 0)),
            pl.BlockSpec((1, ns2, lk), lambda j: (j, 0, 0)),
            pl.BlockSpec((1, 2, ns2 // 2), lambda j: (j, 0, 0)),
        ],
        out_specs=pl.BlockSpec((1, rows, lk), lambda j: (j, 0, 0)),
        scratch_shapes=[pltpu.VMEM((rows, ns2), F32)],
        compiler_params=_params(("parallel",)),
        name=name,
    )(ucat, m_op, wz, wy, a_sub)


def _s5_operators(a_re, a_im, log_dt, b_re, b_im, c_re, c_im):
    g, p = a_re.shape
    c = S5_GROUP
    gpt = LANES // c
    tiles = g // gpt
    ls = S5_SUB

    lam_re, lam_im = a_re.astype(F32), a_im.astype(F32)
    dt = jnp.exp(log_dt.astype(F32))[:, None]
    mag = jnp.exp(lam_re * dt)
    ab_re = mag * jnp.cos(lam_im * dt)
    ab_im = mag * jnp.sin(lam_im * dt)
    den = lam_re * lam_re + lam_im * lam_im
    nr, ni = ab_re - 1.0, ab_im
    coef_re = (nr * lam_re + ni * lam_im) / den
    coef_im = (ni * lam_re - nr * lam_im) / den
    br, bi = b_re.astype(F32), b_im.astype(F32)
    bb_re = coef_re[..., None] * br - coef_im[..., None] * bi
    bb_im = coef_re[..., None] * bi + coef_im[..., None] * br
    cr, ci = c_re.astype(F32), c_im.astype(F32)

    pw_re, pw_im = [jnp.ones_like(ab_re)], [jnp.zeros_like(ab_im)]
    for _ in range(ls):
        r0, i0 = pw_re[-1], pw_im[-1]
        pw_re.append(r0 * ab_re - i0 * ab_im)
        pw_im.append(r0 * ab_im + i0 * ab_re)
    pw_re, pw_im = jnp.stack(pw_re), jnp.stack(pw_im)

    eye = jnp.eye(gpt, dtype=F32)

    def block_diag(t, rows_per_group, cols_per_group):
        lead = t.shape[:-3]
        t = t.reshape(*lead, tiles, gpt, rows_per_group, cols_per_group)
        t = jnp.einsum('...tgrc,gh->...tgrhc', t, eye)
        return t.reshape(*lead, tiles, gpt * rows_per_group, gpt * cols_per_group)

    cb_re = jnp.einsum('gop,tgp->tgop', cr, pw_re[:ls]) - jnp.einsum('gop,tgp->tgop', ci, pw_im[:ls])
    cb_im = jnp.einsum('gop,tgp->tgop', cr, pw_im[:ls]) + jnp.einsum('gop,tgp->tgop', ci, pw_re[:ls])
    kern = (jnp.einsum('tgop,gpi->tgio', cb_re, bb_re, precision='highest')
            - jnp.einsum('tgop,gpi->tgio', cb_im, bb_im, precision='highest'))
    kern = block_diag(kern, c, c)
    zero_blk = jnp.zeros_like(kern[0])
    m_rows = []
    for s in range(ls):
        m_rows.append(jnp.concatenate([kern[t - s] if t >= s else zero_blk for t in range(ls)], axis=-1))
    m_op = jnp.concatenate(m_rows, axis=-2)

    rev_re, rev_im = pw_re[ls - 1::-1], pw_im[ls - 1::-1]
    wz_re = jnp.einsum('sgp,gpi->sgip', rev_re, bb_re) - jnp.einsum('sgp,gpi->sgip', rev_im, bb_im)
    wz_im = jnp.einsum('sgp,gpi->sgip', rev_re, bb_im) + jnp.einsum('sgp,gpi->sgip', rev_im, bb_re)
    wz_re, wz_im = block_diag(wz_re, c, p), block_diag(wz_im, c, p)
    wz = jnp.concatenate([wz_re, wz_im], axis=-1)
    wz = wz.transpose(1, 0, 2, 3).reshape(tiles, ls * LANES, 2 * gpt * p)

    wy_re = jnp.einsum('gop,tgp->tgpo', cr, pw_re[1:]) - jnp.einsum('gop,tgp->tgpo', ci, pw_im[1:])
    wy_im = -(jnp.einsum('gop,tgp->tgpo', cr, pw_im[1:]) + jnp.einsum('gop,tgp->tgpo', ci, pw_re[1:]))
    wy_re, wy_im = block_diag(wy_re, p, c), block_diag(wy_im, p, c)
    wy = jnp.concatenate([wy_re, wy_im], axis=-2)
    wy = wy.transpose(1, 2, 0, 3).reshape(tiles, 2 * gpt * p, ls * LANES)

    a_sub = jnp.stack([pw_re[ls].reshape(tiles, gpt * p), pw_im[ls].reshape(tiles, gpt * p)], axis=1)
    return m_op.astype(BF16), wz.astype(BF16), wy.astype(BF16), a_sub


def _s5_glu_kernel(xf_ref, y_ref, g_ref, d_ref, wv_ref, wg_ref, xb_ref, o_ref, act_ref):
    @pl.when(pl.program_id(1) == 0)
    def _():
        rows_total = xf_ref.shape[0]
        step = min(NORM_ROWS, rows_total)
        g = g_ref[...]
        dsk = d_ref[...]

        def body(r, carry):
            rows = pl.ds(pl.multiple_of(r * step, step), step)
            hn = _rmsnorm(xf_ref[rows, :], g)
            act_ref[rows, :] = jax.nn.gelu(y_ref[rows, :] + dsk * hn).astype(act_ref.dtype)
            return carry

        lax.fori_loop(0, rows_total // step, body, 0)

    act = act_ref[...]
    val = jnp.dot(act, wv_ref[...], preferred_element_type=F32)
    gate = jnp.dot(act, wg_ref[...], preferred_element_type=F32)
    o_ref[...] = xb_ref[...] + val * jax.nn.sigmoid(gate)


def s5_glu_residual(x, y, g, d_skip, w_glu, *, tm=512, tn=512, name):
    m, d = x.shape
    tm, tn = _fit(m, tm), _fit(d, tn)
    nb = d // tn
    return pl.pallas_call(
        _s5_glu_kernel,
        out_shape=jax.ShapeDtypeStruct((m, d), F32),
        grid=(m // tm, nb),
        in_specs=[
            pl.BlockSpec((tm, d), lambda i, j: (i, 0)),
            pl.BlockSpec((tm, d), lambda i, j: (i, 0)),
            pl.BlockSpec((1, d), lambda i, j: (0, 0)),
            pl.BlockSpec((1, d), lambda i, j: (0, 0)),
            pl.BlockSpec((d, tn), lambda i, j: (0, j)),
            pl.BlockSpec((d, tn), lambda i, j: (0, nb + j)),
            pl.BlockSpec((tm, tn), lambda i, j: (i, j)),
        ],
        out_specs=pl.BlockSpec((tm, tn), lambda i, j: (i, j)),
        scratch_shapes=[pltpu.VMEM((tm, d), BF16)],
        compiler_params=_params(("parallel", "arbitrary")),
        name=name,
    )(x, y, g.reshape(1, d), d_skip.reshape(1, d), w_glu, w_glu, x)


def s5_mixer_residual(x, g, a_re, a_im, log_dt, b_re, b_im, c_re, c_im, d_skip, w_glu, *, batch, seq, tag):
    m, d = x.shape
    tiles = d // LANES
    n_sub = seq // (2 * S5_SUB)
    h = rmsnorm_bf16(x, g, name=f"s5_norm_{tag}")
    u = h.reshape(batch, 2, n_sub, S5_SUB, tiles, LANES).transpose(4, 2, 0, 1, 3, 5)
    u = u.reshape(tiles, n_sub * S5_SEQS, S5_SUB * LANES)
    m_op, wz, wy, a_sub = _s5_operators(a_re, a_im, log_dt, b_re, b_im, c_re, c_im)
    y = s5_core(u, m_op, wz, wy, a_sub, name=f"s5_scan_{tag}")
    y = y.reshape(tiles, n_sub, batch, 2, S5_SUB, LANES).transpose(2, 3, 1, 4, 0, 5).reshape(m, d)
    return s5_glu_residual(x, y, g, d_skip, w_glu, name=f"s5_glu_{tag}")


def kernel(x, norm_mix, norm_mlp, norm_final, a_w_in, a_lambda, a_subln, a_w_out, b_a_re, b_a_im, b_log_dt, b_b_re, b_b_im, b_c_re, b_c_im, b_d, b_w_glu, c_w_in, c_w_out, mlp_w1, mlp_w2):
    batch, seq, d = x.shape
    depth = norm_mix.shape[0]
    assert batch * 2 == S5_SEQS, "S5 scan layout places batch x two time halves on the 8 sublanes"
    xf = x.reshape(batch * seq, d)

    for i in range(depth):
        kind = i % N_MIXERS
        j = i // N_MIXERS
        if kind == 0:
            lambda_init = 0.8 - 0.6 * math.exp(-0.3 * i)
            qkv = norm_matmul(xf, norm_mix[i], a_w_in[j].astype(BF16), scaled_cols=d, scale=ATTN_Q_SCALE,
                              name=f"attn_in_{i}")
            o = diff_attention_core(qkv, a_lambda[j], a_subln[j], batch=batch, seq=seq,
                                    lambda_init=lambda_init, name=f"attn_core_{i}")
            xf = matmul_residual(o, a_w_out[j].astype(BF16), xf, name=f"attn_out_{i}")
        elif kind == 1:
            xf = s5_mixer_residual(xf, norm_mix[i], b_a_re[j], b_a_im[j], b_log_dt[j], b_b_re[j],
                                   b_b_im[j], b_c_re[j], b_c_im[j], b_d[j], b_w_glu[j].astype(BF16),
                                   batch=batch, seq=seq, tag=str(i))
        else:
            proj = norm_matmul(xf, norm_mix[i], c_w_in[j].astype(BF16), name=f"ret_in_{i}")
            o = retention_core(proj, batch=batch, seq=seq, name=f"ret_core_{i}")
            xf = matmul_residual(o, c_w_out[j].astype(BF16), xf, name=f"ret_out_{i}")
        xf = mlp_residual(xf, norm_mlp[i], mlp_w1[i].astype(BF16), mlp_w2[i].astype(BF16), norm_final,
                          final_norm=(i == depth - 1), name=f"mlp_{i}")
    return xf.reshape(batch, seq, d)
```

```python
import math
from functools import partial

import jax
import jax.numpy as jnp
from jax import lax
from jax.experimental import pallas as pl
from jax.experimental.pallas import tpu as pltpu

F32 = jnp.float32
BF16 = jnp.bfloat16

EPS = 1e-6
CHUNK = 64
N_MIXERS = 3
DA_HEAD_DIM = 128
DA_VALUE_DIM = 2 * DA_HEAD_DIM
RET_QK_DIM = 256
RET_V_DIM = 2 * RET_QK_DIM
ROPE_BASE = 10000.0
S5_GROUP = 16
S5_STATE = 64

LANES = 128
SUBLANES = 8
VMEM_LIMIT_BYTES = 56 * 1024 * 1024
LOG2E = math.log2(math.e)
NEG_BIG = -1e30

NORM_ROWS = 256
S5_SUB = 8
S5_SEQS = SUBLANES


def _params(semantics):
    return pltpu.CompilerParams(dimension_semantics=semantics, vmem_limit_bytes=VMEM_LIMIT_BYTES)


def _fit(n, preferred):
    t = min(preferred, n)
    while n % t:
        t //= 2
    return t


def _rmsnorm(x, g):
    y = x * lax.rsqrt(jnp.mean(x * x, axis=-1, keepdims=True) + EPS)
    return y * g


def _norm_block_to(x_ref, g_ref, dst_ref):
    rows_total = x_ref.shape[0]
    step = min(NORM_ROWS, rows_total)
    g = g_ref[...]

    def body(r, carry):
        rows = pl.ds(pl.multiple_of(r * step, step), step)
        dst_ref[rows, :] = _rmsnorm(x_ref[rows, :], g).astype(dst_ref.dtype)
        return carry

    lax.fori_loop(0, rows_total // step, body, 0)


def _norm_matmul_kernel(x_ref, g_ref, w_ref, o_ref, hn_ref, *, scaled_blocks, scale):
    j = pl.program_id(1)

    @pl.when(j == 0)
    def _():
        _norm_block_to(x_ref, g_ref, hn_ref)

    acc = jnp.dot(hn_ref[...], w_ref[...], preferred_element_type=F32)
    if scaled_blocks:
        acc = acc * jnp.where(j < scaled_blocks, scale, 1.0)
    o_ref[...] = acc.astype(o_ref.dtype)


def norm_matmul(x, g, w, layer, *, scaled_cols=0, scale=1.0, tm=1024, tn=1024, name):
    m, d = x.shape
    n = w.shape[2]
    tm, tn = _fit(m, tm), _fit(n, tn)
    assert scaled_cols % tn == 0
    return pl.pallas_call(
        partial(_norm_matmul_kernel, scaled_blocks=scaled_cols // tn, scale=scale),
        out_shape=jax.ShapeDtypeStruct((m, n), BF16),
        grid=(m // tm, n // tn),
        in_specs=[
            pl.BlockSpec((tm, d), lambda i, j: (i, 0)),
            pl.BlockSpec((1, d), lambda i, j: (0, 0)),
            pl.BlockSpec((None, d, tn), lambda i, j: (layer, 0, j)),
        ],
        out_specs=pl.BlockSpec((tm, tn), lambda i, j: (i, j)),
        scratch_shapes=[pltpu.VMEM((tm, d), BF16)],
        compiler_params=_params(("parallel", "arbitrary")),
        name=name,
    )(x, g.reshape(1, d), w)


def _matmul_residual_kernel(a_ref, w_ref, x_ref, o_ref):
    o_ref[...] = x_ref[...] + jnp.dot(a_ref[...], w_ref[...], preferred_element_type=F32)


def matmul_residual(a, w, layer, x, *, tm=1024, tn=512, name):
    m, k = a.shape
    n = w.shape[2]
    tm, tn = _fit(m, tm), _fit(n, tn)
    return pl.pallas_call(
        _matmul_residual_kernel,
        out_shape=jax.ShapeDtypeStruct((m, n), F32),
        grid=(m // tm, n // tn),
        in_specs=[
            pl.BlockSpec((tm, k), lambda i, j: (i, 0)),
            pl.BlockSpec((None, k, tn), lambda i, j: (layer, 0, j)),
            pl.BlockSpec((tm, tn), lambda i, j: (i, j)),
        ],
        out_specs=pl.BlockSpec((tm, tn), lambda i, j: (i, j)),
        compiler_params=_params(("parallel", "arbitrary")),
        name=name,
    )(a, w, x)


def _mlp_kernel(x_ref, g_ref, w1_ref, w2_ref, gf_ref, o_ref, hn_ref, *, final_norm):
    f = pl.program_id(1)

    @pl.when(f == 0)
    def _():
        _norm_block_to(x_ref, g_ref, hn_ref)
        o_ref[...] = x_ref[...]

    a = jnp.dot(hn_ref[...], w1_ref[...], preferred_element_type=F32)
    a = jnp.maximum(a, 0.0)
    a = (a * a).astype(BF16)
    o_ref[...] += jnp.dot(a, w2_ref[...], preferred_element_type=F32)

    if final_norm:
        @pl.when(f == pl.num_programs(1) - 1)
        def _():
            _norm_block_to(o_ref, gf_ref, o_ref)


def mlp_residual(x, g, w1, w2, layer, g_final, *, final_norm, tm=512, tf=1024, name):
    m, d = x.shape
    ff = w1.shape[2]
    tm, tf = _fit(m, tm), _fit(ff, tf)
    return pl.pallas_call(
        partial(_mlp_kernel, final_norm=final_norm),
        out_shape=jax.ShapeDtypeStruct((m, d), F32),
        grid=(m // tm, ff // tf),
        in_specs=[
            pl.BlockSpec((tm, d), lambda i, f: (i, 0)),
            pl.BlockSpec((1, d), lambda i, f: (0, 0)),
            pl.BlockSpec((None, d, tf), lambda i, f: (layer, 0, f)),
            pl.BlockSpec((None, tf, d), lambda i, f: (layer, f, 0)),
            pl.BlockSpec((1, d), lambda i, f: (0, 0)),
        ],
        out_specs=pl.BlockSpec((tm, d), lambda i, f: (i, 0)),
        scratch_shapes=[pltpu.VMEM((tm, d), BF16)],
        compiler_params=_params(("parallel", "arbitrary")),
        name=name,
    )(x, g.reshape(1, d), w1, w2, g_final.reshape(1, d))


def _attn_kernel(lam_ref, g_ref, q_ref, k_ref, v_ref, o_ref, qs_ref, m_ref, l_ref, alpha_ref, acc_ref, p_ref,
                 sa_ref, sb_ref, *, tq, tk, row_chunk, lambda_init):
    qi = pl.program_id(2)
    hd = DA_HEAD_DIM
    lane_reps = tk // LANES

    zeros = jnp.zeros((tq, hd), qs_ref.dtype)
    qs_ref[0:tq, 0:hd] = q_ref[:, 0:hd]
    qs_ref[0:tq, hd:] = zeros
    qs_ref[tq:, 0:hd] = zeros
    qs_ref[tq:, hd:] = q_ref[:, hd:]

    m_ref[...] = jnp.full(m_ref.shape, NEG_BIG, F32)
    l_ref[...] = jnp.zeros(l_ref.shape, F32)
    acc_ref[...] = jnp.zeros(acc_ref.shape, F32)

    def scores_to(s_ref, blk):
        start = pl.multiple_of(blk * tk, tk)
        s_ref[...] = lax.dot_general(qs_ref[...], k_ref[pl.ds(start, tk), :], (((1,), (1,)), ((), ())),
                                     preferred_element_type=F32)

    def absorb(s_ref, blk, masked):
        start = pl.multiple_of(blk * tk, tk)
        for r0 in range(0, 2 * tq, row_chunk):
            rows = slice(r0, r0 + row_chunk)
            s = s_ref[rows, :]
            if masked:
                q_pos = lax.broadcasted_iota(jnp.int32, (row_chunk, tk), 0) + (r0 % tq + qi * tq)
                k_pos = lax.broadcasted_iota(jnp.int32, (row_chunk, tk), 1) + start
                s = jnp.where(k_pos // CHUNK <= q_pos // CHUNK, s, NEG_BIG)
            m_prev = m_ref[rows, :]
            m_new = jnp.maximum(m_prev, jnp.max(s, axis=-1, keepdims=True))
            alpha = jnp.exp2(m_prev - m_new)
            p = jnp.exp2(s - jnp.tile(m_new, (1, lane_reps)))
            l_ref[rows, :] = alpha * l_ref[rows, :] + jnp.sum(p, axis=-1, keepdims=True)
            p_ref[rows, :] = p.astype(BF16)
            alpha_ref[rows, :] = alpha
            m_ref[rows, :] = m_new
        pv = jnp.dot(p_ref[...], v_ref[pl.ds(start, tk), :], preferred_element_type=F32)
        acc_ref[...] = jnp.tile(alpha_ref[...], (1, DA_VALUE_DIM // LANES)) * acc_ref[...] + pv

    n_full = (qi * tq) // tk
    n_pairs = n_full // 2
    scores_to(sa_ref, 0)

    def pair_step(jj, carry):
        j = 2 * jj
        absorb(sa_ref, j, False)
        scores_to(sb_ref, j + 1)
        absorb(sb_ref, j + 1, False)
        scores_to(sa_ref, j + 2)
        return carry

    lax.fori_loop(0, n_pairs, pair_step, 0)

    @pl.when(n_full % 2 == 0)
    def _():
        absorb(sa_ref, n_full, True)

    @pl.when(n_full % 2 == 1)
    def _():
        absorb(sa_ref, n_full - 1, False)
        scores_to(sb_ref, n_full)
        absorb(sb_ref, n_full, True)

    lp = lam_ref[...]
    lam = (jnp.exp(jnp.sum(lp[0:1] * lp[1:2], axis=-1, keepdims=True))
           - jnp.exp(jnp.sum(lp[2:3] * lp[3:4], axis=-1, keepdims=True)) + lambda_init)
    inv_l = jnp.tile(1.0 / l_ref[...], (1, DA_VALUE_DIM // LANES))
    o = acc_ref[...] * inv_l
    o = o[:tq] - lam * o[tq:]
    o = o * lax.rsqrt(jnp.mean(o * o, axis=-1, keepdims=True) + EPS)
    o = o * g_ref[...] * (1.0 - lambda_init)
    o_ref[...] = o.astype(o_ref.dtype)


ATTN_Q_SCALE = DA_HEAD_DIM ** -0.5 * LOG2E


def diff_attention_core(qkv, lam_p, subln_g, *, batch, seq, lambda_init, tq=512, tk=512, row_chunk=64, name):
    m, d3 = qkv.shape
    d = d3 // 3
    heads = d // DA_VALUE_DIM
    tq = _fit(seq, tq)
    tk = _fit(seq, max(tk, tq))
    assert tk % tq == 0 and tq % row_chunk == 0
    nq = seq // tq
    return pl.pallas_call(
        partial(_attn_kernel, tq=tq, tk=tk, row_chunk=row_chunk, lambda_init=lambda_init),
        out_shape=jax.ShapeDtypeStruct((m, d), BF16),
        grid=(batch, heads, nq),
        in_specs=[
            pl.BlockSpec((4, DA_HEAD_DIM), lambda b, h, i: (0, 0)),
            pl.BlockSpec((1, DA_VALUE_DIM), lambda b, h, i: (0, 0)),
            pl.BlockSpec((tq, DA_VALUE_DIM), lambda b, h, i: (b * nq + i, h)),
            pl.BlockSpec((seq, DA_VALUE_DIM), lambda b, h, i: (b, heads + h)),
            pl.BlockSpec((seq, DA_VALUE_DIM), lambda b, h, i: (b, 2 * heads + h)),
        ],
        out_specs=pl.BlockSpec((tq, DA_VALUE_DIM), lambda b, h, i: (b * nq + i, h)),
        scratch_shapes=[pltpu.VMEM((2 * tq, DA_VALUE_DIM), BF16), pltpu.VMEM((2 * tq, LANES), F32),
                        pltpu.VMEM((2 * tq, LANES), F32), pltpu.VMEM((2 * tq, LANES), F32),
                        pltpu.VMEM((2 * tq, DA_VALUE_DIM), F32), pltpu.VMEM((2 * tq, tk), BF16),
                        pltpu.VMEM((2 * tq, tk), F32), pltpu.VMEM((2 * tq, tk), F32)],
        compiler_params=_params(("parallel", "parallel", "arbitrary")),
        name=name,
    )(lam_p, subln_g.reshape(1, DA_VALUE_DIM), qkv, qkv, qkv)


def _retention_kernel(lg_ref, cos_ref, sin_ref, q_ref, k_ref, v_ref, g_ref, o_ref, r_ref, *, lc):
    h = pl.program_id(1)
    half = RET_QK_DIM // 2

    @pl.when(pl.program_id(2) == 0)
    def _():
        r_ref[...] = jnp.zeros_like(r_ref)

    lg = lg_ref[h]
    cos = cos_ref[...]
    sin = sin_ref[...]

    def rope(t):
        t = t.astype(F32)
        t1, t2 = t[:, :half], t[:, half:]
        return jnp.concatenate([t1 * cos - t2 * sin, t1 * sin + t2 * cos], axis=-1)

    qr = rope(q_ref[...])
    kr = rope(k_ref[...]) * RET_QK_DIM ** -0.5
    v = v_ref[...]

    pos = lax.broadcasted_iota(jnp.int32, (lc, 1), 0).astype(F32)
    q_decay = jnp.exp(lg * (pos + 1.0))
    k_decay = jnp.exp(lg * (lc - 1.0 - pos))
    ni = lax.broadcasted_iota(jnp.int32, (lc, lc), 0)
    mi = lax.broadcasted_iota(jnp.int32, (lc, lc), 1)
    decay = jnp.where(mi // CHUNK <= ni // CHUNK,
                      jnp.exp(lg * jnp.abs(ni - mi).astype(F32)), 0.0)

    s = lax.dot_general(qr.astype(BF16), kr.astype(BF16), (((1,), (1,)), ((), ())),
                        preferred_element_type=F32) * decay
    r = r_ref[...]
    o = jnp.dot(s.astype(BF16), v, preferred_element_type=F32)
    o = o + jnp.dot((qr * q_decay).astype(BF16), r.astype(BF16), preferred_element_type=F32)
    r_ref[...] = r * jnp.exp(lg * lc) + lax.dot_general(
        (kr * k_decay).astype(BF16), v, (((0,), (0,)), ((), ())), preferred_element_type=F32)

    o = o * lax.rsqrt(jnp.mean(o * o, axis=-1, keepdims=True) + EPS)
    gate = g_ref[...].astype(F32)
    o_ref[...] = (gate * jax.nn.sigmoid(gate) * o).astype(o_ref.dtype)


def retention_core(proj, *, batch, seq, lc=256, name):
    m, d6 = proj.shape
    d = d6 // 6
    heads = d // RET_QK_DIM
    lc = _fit(seq, lc)
    nc = seq // lc
    half = RET_QK_DIM // 2
    log_gamma = jnp.log(1.0 - jnp.exp2(-5.0 - jnp.arange(heads, dtype=F32)))
    inv = 1.0 / (ROPE_BASE ** jnp.linspace(0.0, 1.0, half, dtype=F32))
    ang = jnp.arange(seq, dtype=F32)[:, None] * inv[None, :]
    cos, sin = jnp.cos(ang), jnp.sin(ang)
    vmem = pltpu.VMEM
    return pl.pallas_call(
        partial(_retention_kernel, lc=lc),
        out_shape=jax.ShapeDtypeStruct((m, 2 * d), BF16),
        grid=(batch, heads, nc),
        in_specs=[
            pl.BlockSpec(memory_space=pltpu.SMEM),
            pl.BlockSpec((lc, half), lambda b, h, c: (c, 0)),
            pl.BlockSpec((lc, half), lambda b, h, c: (c, 0)),
            pl.BlockSpec((lc, RET_QK_DIM), lambda b, h, c: (b * nc + c, h)),
            pl.BlockSpec((lc, RET_QK_DIM), lambda b, h, c: (b * nc + c, heads + h)),
            pl.BlockSpec((lc, RET_V_DIM), lambda b, h, c: (b * nc + c, heads + h)),
            pl.BlockSpec((lc, RET_V_DIM), lambda b, h, c: (b * nc + c, 2 * heads + h)),
        ],
        out_specs=pl.BlockSpec((lc, RET_V_DIM), lambda b, h, c: (b * nc + c, h)),
        scratch_shapes=[vmem((RET_QK_DIM, RET_V_DIM), F32)],
        compiler_params=_params(("parallel", "parallel", "arbitrary")),
        name=name,
    )(log_gamma, cos, sin, proj, proj, proj, proj)


def _s5_norm_kernel(x_ref, g_ref, o_ref, slab_ref, asm_ref):
    tt = x_ref.shape[1]
    tiles = o_ref.shape[0]
    nk = tt // S5_SUB
    g = g_ref[...]
    for q in range(S5_SEQS):
        hn = _rmsnorm(x_ref[q], g)
        for j in range(tiles):
            slab_ref[...] = hn[:, j * LANES:(j + 1) * LANES]
            for s in range(S5_SUB):
                asm_ref[j, s, pl.ds(q, nk, stride=S5_SEQS), :] = slab_ref[pl.ds(s, nk, stride=S5_SUB), :]
    o_ref[...] = asm_ref[...].astype(o_ref.dtype)


def s5_norm_to_scan_layout(x, g, *, batch, seq, tt=64, name):
    m, d = x.shape
    tiles = d // LANES
    half = seq // 2
    tt = _fit(half, tt)
    return pl.pallas_call(
        _s5_norm_kernel,
        out_shape=jax.ShapeDtypeStruct((tiles, S5_SUB, m // S5_SUB, LANES), BF16),
        grid=(half // tt,),
        in_specs=[pl.BlockSpec((S5_SEQS, tt, d), lambda i: (0, i, 0)), pl.BlockSpec((1, d), lambda i: (0, 0))],
        out_specs=pl.BlockSpec((tiles, S5_SUB, tt, LANES), lambda i: (0, 0, i, 0)),
        scratch_shapes=[pltpu.VMEM((tt, LANES), F32), pltpu.VMEM((tiles, S5_SUB, tt, LANES), F32)],
        compiler_params=_params(("parallel",)),
        name=name,
    )(x.reshape(S5_SEQS, half, d), g.reshape(1, d))


def _s5_kernel(u_ref, kc_ref, wzc_ref, wyc_ref, a_ref, y_ref, ucat_ref, m_ref, wz_ref, wy_ref, z_ref, *,
               row_chunk):
    rows = u_ref.shape[2]
    ns = z_ref.shape[1] // 2
    n_sub = rows // S5_SEQS
    n_chunks = rows // row_chunk
    grp = LANES // S5_GROUP

    for s in range(S5_SUB):
        ucat_ref[:, s * LANES:(s + 1) * LANES] = u_ref[0, s]

    def same_group(shape, row_span, col_span):
        return (lax.broadcasted_iota(jnp.int32, shape, 0) // row_span
                == lax.broadcasted_iota(jnp.int32, shape, 1) // col_span)

    diag_k = same_group((LANES, LANES), S5_GROUP, S5_GROUP)
    zero_blk = jnp.zeros((LANES, LANES), m_ref.dtype)
    for tau in range(S5_SUB):
        blk = jnp.where(diag_k, jnp.tile(kc_ref[0, tau], (grp, 1)), 0.0).astype(m_ref.dtype)
        for s in range(S5_SUB - tau):
            t = s + tau
            m_ref[s * LANES:(s + 1) * LANES, t * LANES:(t + 1) * LANES] = blk
    for s in range(S5_SUB):
        for t in range(s):
            m_ref[s * LANES:(s + 1) * LANES, t * LANES:(t + 1) * LANES] = zero_blk
    diag_z = same_group((LANES, ns), S5_GROUP, S5_STATE)
    diag_y = same_group((ns, LANES), S5_STATE, S5_GROUP)
    for s in range(S5_SUB):
        for part in range(2):
            wz_ref[s * LANES:(s + 1) * LANES, part * ns:(part + 1) * ns] = jnp.where(
                diag_z, jnp.tile(wzc_ref[0, s, part], (grp, 1)), 0.0).astype(wz_ref.dtype)
            wy_ref[part * ns:(part + 1) * ns, s * LANES:(s + 1) * LANES] = jnp.where(
                diag_y, jnp.tile(wyc_ref[0, s, part], (grp, 1)), 0.0).astype(wy_ref.dtype)

    def z_body(i, carry):
        rs = pl.ds(pl.multiple_of(i * row_chunk, row_chunk), row_chunk)
        z_ref[rs, :] = jnp.dot(ucat_ref[rs, :], wz_ref[...], preferred_element_type=F32)
        return carry

    lax.fori_loop(0, n_chunks, z_body, 0)

    a_re = jnp.broadcast_to(a_ref[0, 0:1, :], (S5_SEQS, ns))
    a_im = jnp.broadcast_to(a_ref[0, 1:2, :], (S5_SEQS, ns))

    def advance(k, sr, si):
        rs = pl.ds(pl.multiple_of(k * S5_SEQS, S5_SEQS), S5_SEQS)
        zr = z_ref[rs, :ns]
        zi = z_ref[rs, ns:]
        return rs, a_re * sr - a_im * si + zr, a_re * si + a_im * sr + zi

    def pass1(k, st):
        _, nr, ni = advance(k, *st)
        return nr, ni

    zero = jnp.zeros((S5_SEQS, ns), F32)
    end_re, end_im = lax.fori_loop(0, n_sub, pass1, (zero, zero))

    odd = lax.broadcasted_iota(jnp.int32, (S5_SEQS, ns), 0) % 2 == 1
    init = (jnp.where(odd, pltpu.roll(end_re, 1, axis=0), 0.0),
            jnp.where(odd, pltpu.roll(end_im, 1, axis=0), 0.0))

    def pass2(k, st):
        rs, nr, ni = advance(k, *st)
        z_ref[rs, :ns] = st[0]
        z_ref[rs, ns:] = st[1]
        return nr, ni

    lax.fori_loop(0, n_sub, pass2, init)

    def y_body(i, carry):
        rs = pl.ds(pl.multiple_of(i * row_chunk, row_chunk), row_chunk)
        y = (jnp.dot(ucat_ref[rs, :], m_ref[...], preferred_element_type=F32)
             + jnp.dot(z_ref[rs, :].astype(BF16), wy_ref[...], preferred_element_type=F32))
        for s in range(S5_SUB):
            y_ref[0, s, rs, :] = y[:, s * LANES:(s + 1) * LANES]
        return carry

    lax.fori_loop(0, n_chunks, y_body, 0)


def s5_core(u, kc, wzc, wyc, a_sub, *, row_chunk=512, name):
    tiles, sub, rows, _ = u.shape
    ns = a_sub.shape[2]
    lk = sub * LANES
    row_chunk = _fit(rows, row_chunk)
    return pl.pallas_call(
        partial(_s5_kernel, row_chunk=row_chunk),
        out_shape=jax.ShapeDtypeStruct((tiles, sub, rows, LANES), F32),
        grid=(tiles,),
        in_specs=[
            pl.BlockSpec((1, sub, rows, LANES), lambda j: (j, 0, 0, 0)),
            pl.BlockSpec((1, sub, S5_GROUP, LANES), lambda j: (j, 0, 0, 0)),
            pl.BlockSpec((1, sub, 2, S5_GROUP, ns), lambda j: (j, 0, 0, 0, 0)),
            pl.BlockSpec((1, sub, 2, S5_STATE, LANES), lambda j: (j, 0, 0, 0, 0)),
            pl.BlockSpec((1, 2, ns), lambda j: (j, 0, 0)),
        ],
        out_specs=pl.BlockSpec((1, sub, rows, LANES), lambda j: (j, 0, 0, 0)),
        scratch_shapes=[pltpu.VMEM((rows, lk), BF16), pltpu.VMEM((lk, lk), BF16), pltpu.VMEM((lk, 2 * ns), BF16),
                        pltpu.VMEM((2 * ns, lk), BF16), pltpu.VMEM((rows, 2 * ns), F32)],
        compiler_params=_params(("parallel",)),
        name=name,
    )(u, kc, wzc, wyc, a_sub)


def _s5_operators(a_re, a_im, log_dt, b_re, b_im, c_re, c_im):
    g, p = a_re.shape
    c = S5_GROUP
    gpt = LANES // c
    tiles = g // gpt
    ls = S5_SUB

    lam_re, lam_im = a_re.astype(F32), a_im.astype(F32)
    dt = jnp.exp(log_dt.astype(F32))[:, None]
    mag = jnp.exp(lam_re * dt)
    ab_re = mag * jnp.cos(lam_im * dt)
    ab_im = mag * jnp.sin(lam_im * dt)
    den = lam_re * lam_re + lam_im * lam_im
    nr, ni = ab_re - 1.0, ab_im
    coef_re = (nr * lam_re + ni * lam_im) / den
    coef_im = (ni * lam_re - nr * lam_im) / den
    br, bi = b_re.astype(F32), b_im.astype(F32)
    bb_re = coef_re[..., None] * br - coef_im[..., None] * bi
    bb_im = coef_re[..., None] * bi + coef_im[..., None] * br
    cr, ci = c_re.astype(F32), c_im.astype(F32)

    pw_re, pw_im = [jnp.ones_like(ab_re)], [jnp.zeros_like(ab_im)]
    for _ in range(ls):
        r0, i0 = pw_re[-1], pw_im[-1]
        pw_re.append(r0 * ab_re - i0 * ab_im)
        pw_im.append(r0 * ab_im + i0 * ab_re)
    pw_re, pw_im = jnp.stack(pw_re), jnp.stack(pw_im)

    def per_tile(t, lead):
        r, w = t.shape[-2:]
        n = len(lead)
        t = t.reshape(*lead, tiles, gpt, r, w)
        t = jnp.transpose(t, (n, *range(n), n + 2, n + 1, n + 3))
        return t.reshape(tiles, *lead, r, gpt * w)

    cb_re = jnp.einsum('gop,tgp->tgop', cr, pw_re[:ls]) - jnp.einsum('gop,tgp->tgop', ci, pw_im[:ls])
    cb_im = jnp.einsum('gop,tgp->tgop', cr, pw_im[:ls]) + jnp.einsum('gop,tgp->tgop', ci, pw_re[:ls])
    kern = (jnp.einsum('tgop,gpi->tgio', cb_re, bb_re, precision='highest')
            - jnp.einsum('tgop,gpi->tgio', cb_im, bb_im, precision='highest'))
    kc = per_tile(kern, (ls,))

    rev_re, rev_im = pw_re[ls - 1::-1], pw_im[ls - 1::-1]
    wz_re = jnp.einsum('sgp,gpi->sgip', rev_re, bb_re) - jnp.einsum('sgp,gpi->sgip', rev_im, bb_im)
    wz_im = jnp.einsum('sgp,gpi->sgip', rev_re, bb_im) + jnp.einsum('sgp,gpi->sgip', rev_im, bb_re)
    wzc = per_tile(jnp.stack([wz_re, wz_im], axis=1), (ls, 2))

    wy_re = jnp.einsum('gop,tgp->tgpo', cr, pw_re[1:]) - jnp.einsum('gop,tgp->tgpo', ci, pw_im[1:])
    wy_im = -(jnp.einsum('gop,tgp->tgpo', cr, pw_im[1:]) + jnp.einsum('gop,tgp->tgpo', ci, pw_re[1:]))
    wyc = per_tile(jnp.stack([wy_re, wy_im], axis=1), (ls, 2))

    a_sub = jnp.stack([pw_re[ls].reshape(tiles, gpt * p), pw_im[ls].reshape(tiles, gpt * p)], axis=1)
    return kc, wzc, wyc, a_sub


def _s5_act_kernel(x_ref, y_ref, g_ref, d_ref, o_ref, slab_ref):
    tt = x_ref.shape[1]
    tiles = y_ref.shape[0]
    nk = tt // S5_SUB
    for q in range(S5_SEQS):
        x = x_ref[q]
        inv_rms = lax.rsqrt(jnp.mean(x * x, axis=-1, keepdims=True) + EPS)
        for j in range(tiles):
            lanes = slice(j * LANES, (j + 1) * LANES)
            for s in range(S5_SUB):
                slab_ref[pl.ds(s, nk, stride=S5_SUB), :] = y_ref[j, s, pl.ds(q, nk, stride=S5_SEQS), :]
            hn = x_ref[q, :, lanes] * inv_rms * g_ref[:, lanes]
            o_ref[q, :, lanes] = jax.nn.gelu(slab_ref[...] + d_ref[:, lanes] * hn).astype(o_ref.dtype)


def s5_activation(x, y_scan, g, d_skip, *, batch, seq, tt=64, name):
    m, d = x.shape
    tiles = d // LANES
    half = seq // 2
    tt = _fit(half, tt)
    act = pl.pallas_call(
        _s5_act_kernel,
        out_shape=jax.ShapeDtypeStruct((S5_SEQS, half, d), BF16),
        grid=(half // tt,),
        in_specs=[
            pl.BlockSpec((S5_SEQS, tt, d), lambda i: (0, i, 0)),
            pl.BlockSpec((tiles, S5_SUB, tt, LANES), lambda i: (0, 0, i, 0)),
            pl.BlockSpec((1, d), lambda i: (0, 0)),
            pl.BlockSpec((1, d), lambda i: (0, 0)),
        ],
        out_specs=pl.BlockSpec((S5_SEQS, tt, d), lambda i: (0, i, 0)),
        scratch_shapes=[pltpu.VMEM((tt, LANES), F32)],
        compiler_params=_params(("parallel",)),
        name=name,
    )(x.reshape(S5_SEQS, half, d), y_scan, g.reshape(1, d), d_skip.reshape(1, d))
    return act.reshape(m, d)


def _glu_residual_kernel(a_ref, wv_ref, wg_ref, x_ref, o_ref):
    a = a_ref[...]
    val = jnp.dot(a, wv_ref[...], preferred_element_type=F32)
    gate = jnp.dot(a, wg_ref[...], preferred_element_type=F32)
    o_ref[...] = x_ref[...] + val * jax.nn.sigmoid(gate)


def glu_residual(a, w_glu, layer, x, *, tm=1024, tn=512, name):
    m, d = x.shape
    tm, tn = _fit(m, tm), _fit(d, tn)
    nb = d // tn
    return pl.pallas_call(
        _glu_residual_kernel,
        out_shape=jax.ShapeDtypeStruct((m, d), F32),
        grid=(m // tm, nb),
        in_specs=[
            pl.BlockSpec((tm, d), lambda i, j: (i, 0)),
            pl.BlockSpec((None, d, tn), lambda i, j: (layer, 0, j)),
            pl.BlockSpec((None, d, tn), lambda i, j: (layer, 0, nb + j)),
            pl.BlockSpec((tm, tn), lambda i, j: (i, j)),
        ],
        out_specs=pl.BlockSpec((tm, tn), lambda i, j: (i, j)),
        compiler_params=_params(("parallel", "arbitrary")),
        name=name,
    )(a, w_glu, w_glu, x)


def s5_mixer_residual(x, g, a_re, a_im, log_dt, b_re, b_im, c_re, c_im, d_skip, w_glu, layer, *, batch, seq,
                      tag):
    u = s5_norm_to_scan_layout(x, g, batch=batch, seq=seq, name=f"s5_norm_{tag}")
    kc, wzc, wyc, a_sub = _s5_operators(a_re, a_im, log_dt, b_re, b_im, c_re, c_im)
    y = s5_core(u, kc, wzc, wyc, a_sub, name=f"s5_scan_{tag}")
    act = s5_activation(x, y, g, d_skip, batch=batch, seq=seq, name=f"s5_act_{tag}")
    return glu_residual(act, w_glu, layer, x, name=f"s5_glu_{tag}")


def kernel(x, norm_mix, norm_mlp, norm_final, a_w_in, a_lambda, a_subln, a_w_out, b_a_re, b_a_im, b_log_dt, b_b_re, b_b_im, b_c_re, b_c_im, b_d, b_w_glu, c_w_in, c_w_out, mlp_w1, mlp_w2):
    batch, seq, d = x.shape
    depth = norm_mix.shape[0]
    assert batch * 2 == S5_SEQS, "S5 scan layout places batch x two time halves on the 8 sublanes"
    xf = x.reshape(batch * seq, d)
    a_w_in, a_w_out, b_w_glu, c_w_in, c_w_out, mlp_w1, mlp_w2 = (
        w.astype(BF16) for w in (a_w_in, a_w_out, b_w_glu, c_w_in, c_w_out, mlp_w1, mlp_w2))

    for i in range(depth):
        kind = i % N_MIXERS
        j = i // N_MIXERS
        if kind == 0:
            lambda_init = 0.8 - 0.6 * math.exp(-0.3 * i)
            qkv = norm_matmul(xf, norm_mix[i], a_w_in, j, scaled_cols=d, scale=ATTN_Q_SCALE, name=f"attn_in_{i}")
            o = diff_attention_core(qkv, a_lambda[j], a_subln[j], batch=batch, seq=seq,
                                    lambda_init=lambda_init, name=f"attn_core_{i}")
            xf = matmul_residual(o, a_w_out, j, xf, name=f"attn_out_{i}")
        elif kind == 1:
            xf = s5_mixer_residual(xf, norm_mix[i], b_a_re[j], b_a_im[j], b_log_dt[j], b_b_re[j],
                                   b_b_im[j], b_c_re[j], b_c_im[j], b_d[j], b_w_glu, j,
                                   batch=batch, seq=seq, tag=str(i))
        else:
            proj = norm_matmul(xf, norm_mix[i], c_w_in, j, name=f"ret_in_{i}")
            o = retention_core(proj, batch=batch, seq=seq, name=f"ret_core_{i}")
            xf = matmul_residual(o, c_w_out, j, xf, name=f"ret_out_{i}")
        xf = mlp_residual(xf, norm_mlp[i], mlp_w1, mlp_w2, i, norm_final,
                          final_norm=(i == depth - 1), name=f"mlp_{i}")
    return xf.reshape(batch, seq, d)
```

```python
import math
from functools import partial

import jax
import jax.numpy as jnp
from jax import lax
from jax.experimental import pallas as pl
from jax.experimental.pallas import tpu as pltpu

F32 = jnp.float32
BF16 = jnp.bfloat16

EPS = 1e-6
CHUNK = 64
N_MIXERS = 3
DA_HEAD_DIM = 128
DA_VALUE_DIM = 2 * DA_HEAD_DIM
RET_QK_DIM = 256
RET_V_DIM = 2 * RET_QK_DIM
ROPE_BASE = 10000.0
S5_GROUP = 16
S5_STATE = 64

LANES = 128
SUBLANES = 8
VMEM_LIMIT_BYTES = 56 * 1024 * 1024
LOG2E = math.log2(math.e)
NEG_BIG = -1e30

NORM_ROWS = 256
S5_SUB = 8
S5_SEQS = SUBLANES


def _params(semantics):
    return pltpu.CompilerParams(dimension_semantics=semantics, vmem_limit_bytes=VMEM_LIMIT_BYTES)


def _fit(n, preferred):
    t = min(preferred, n)
    while n % t:
        t //= 2
    return t


def _rmsnorm(x, g):
    y = x * lax.rsqrt(jnp.mean(x * x, axis=-1, keepdims=True) + EPS)
    return y * g


def _norm_block_to(x_ref, g_ref, dst_ref):
    rows_total = x_ref.shape[0]
    step = min(NORM_ROWS, rows_total)
    g = g_ref[...]

    def body(r, carry):
        rows = pl.ds(pl.multiple_of(r * step, step), step)
        dst_ref[rows, :] = _rmsnorm(x_ref[rows, :], g).astype(dst_ref.dtype)
        return carry

    lax.fori_loop(0, rows_total // step, body, 0)


def _norm_matmul_kernel(x_ref, g_ref, w_ref, *rest, epilogue, scaled_blocks, scale, head_dim):
    if epilogue == "rope":
        cos_ref, sin_ref, o_ref, hn_ref = rest
    else:
        o_ref, hn_ref = rest
    j = pl.program_id(1)

    @pl.when(j == 0)
    def _():
        _norm_block_to(x_ref, g_ref, hn_ref)

    acc = jnp.dot(hn_ref[...], w_ref[...], preferred_element_type=F32)
    if epilogue == "scale":
        if scaled_blocks:
            acc = acc * jnp.where(j < scaled_blocks, scale, 1.0)
        o_ref[...] = acc.astype(o_ref.dtype)
    elif epilogue == "silu":
        o_ref[...] = (acc * jax.nn.sigmoid(acc)).astype(o_ref.dtype)
    else:
        half = head_dim // 2
        cos = cos_ref[...]
        sin = sin_ref[...]
        sc = jnp.where(j < scaled_blocks, 1.0, scale)
        for c0 in range(0, acc.shape[1], head_dim):
            t1 = acc[:, c0:c0 + half]
            t2 = acc[:, c0 + half:c0 + head_dim]
            o_ref[:, c0:c0 + half] = ((t1 * cos - t2 * sin) * sc).astype(o_ref.dtype)
            o_ref[:, c0 + half:c0 + head_dim] = ((t1 * sin + t2 * cos) * sc).astype(o_ref.dtype)


def norm_matmul(x, g, w, layer, *, col_start=0, n_cols=None, epilogue="scale", scaled_cols=0, scale=1.0,
                rope=None, seq=None, head_dim=None, tm=1024, tn=1024, name):
    m, d = x.shape
    n = w.shape[2] - col_start if n_cols is None else n_cols
    tm, tn = _fit(m, tm), _fit(math.gcd(n, scaled_cols, col_start), tn)
    jb = col_start // tn
    in_specs = [
        pl.BlockSpec((tm, d), lambda i, j: (i, 0)),
        pl.BlockSpec((1, d), lambda i, j: (0, 0)),
        pl.BlockSpec((None, d, tn), lambda i, j: (layer, 0, jb + j)),
    ]
    args = [x, g.reshape(1, d), w]
    if epilogue == "rope":
        assert seq % tm == 0 and tn % head_dim == 0
        pos_blocks = seq // tm
        in_specs += [pl.BlockSpec((tm, head_dim // 2), lambda i, j: (i % pos_blocks, 0))] * 2
        args += list(rope)
    return pl.pallas_call(
        partial(_norm_matmul_kernel, epilogue=epilogue, scaled_blocks=scaled_cols // tn, scale=scale,
                head_dim=head_dim),
        out_shape=jax.ShapeDtypeStruct((m, n), BF16),
        grid=(m // tm, n // tn),
        in_specs=in_specs,
        out_specs=pl.BlockSpec((tm, tn), lambda i, j: (i, j)),
        scratch_shapes=[pltpu.VMEM((tm, d), BF16)],
        compiler_params=_params(("parallel", "arbitrary")),
        name=name,
    )(*args)


def _matmul_residual_kernel(a_ref, w_ref, x_ref, o_ref):
    o_ref[...] = x_ref[...] + jnp.dot(a_ref[...], w_ref[...], preferred_element_type=F32)


def matmul_residual(a, w, layer, x, *, tm=1024, tn=512, name):
    m, k = a.shape
    n = w.shape[2]
    tm, tn = _fit(m, tm), _fit(n, tn)
    return pl.pallas_call(
        _matmul_residual_kernel,
        out_shape=jax.ShapeDtypeStruct((m, n), F32),
        grid=(m // tm, n // tn),
        in_specs=[
            pl.BlockSpec((tm, k), lambda i, j: (i, 0)),
            pl.BlockSpec((None, k, tn), lambda i, j: (layer, 0, j)),
            pl.BlockSpec((tm, tn), lambda i, j: (i, j)),
        ],
        out_specs=pl.BlockSpec((tm, tn), lambda i, j: (i, j)),
        compiler_params=_params(("parallel", "arbitrary")),
        name=name,
    )(a, w, x)


def _mlp_kernel(x_ref, g_ref, w1_ref, w2_ref, gf_ref, o_ref, hn_ref, *, final_norm):
    f = pl.program_id(1)

    @pl.when(f == 0)
    def _():
        _norm_block_to(x_ref, g_ref, hn_ref)
        o_ref[...] = x_ref[...]

    a = jnp.dot(hn_ref[...], w1_ref[...], preferred_element_type=F32)
    a = jnp.maximum(a, 0.0)
    a = (a * a).astype(BF16)
    o_ref[...] += jnp.dot(a, w2_ref[...], preferred_element_type=F32)

    if final_norm:
        @pl.when(f == pl.num_programs(1) - 1)
        def _():
            _norm_block_to(o_ref, gf_ref, o_ref)


def mlp_residual(x, g, w1, w2, layer, g_final, *, final_norm, tm=512, tf=1024, name):
    m, d = x.shape
    ff = w1.shape[2]
    tm, tf = _fit(m, tm), _fit(ff, tf)
    return pl.pallas_call(
        partial(_mlp_kernel, final_norm=final_norm),
        out_shape=jax.ShapeDtypeStruct((m, d), F32),
        grid=(m // tm, ff // tf),
        in_specs=[
            pl.BlockSpec((tm, d), lambda i, f: (i, 0)),
            pl.BlockSpec((1, d), lambda i, f: (0, 0)),
            pl.BlockSpec((None, d, tf), lambda i, f: (layer, 0, f)),
            pl.BlockSpec((None, tf, d), lambda i, f: (layer, f, 0)),
            pl.BlockSpec((1, d), lambda i, f: (0, 0)),
        ],
        out_specs=pl.BlockSpec((tm, d), lambda i, f: (i, 0)),
        scratch_shapes=[pltpu.VMEM((tm, d), BF16)],
        compiler_params=_params(("parallel", "arbitrary")),
        name=name,
    )(x, g.reshape(1, d), w1, w2, g_final.reshape(1, d))


def _attn_kernel(lam_ref, g_ref, q_ref, k_ref, v_ref, o_ref, qs_ref, m_ref, l_ref, alpha_ref, acc_ref, p_ref,
                 sa_ref, sb_ref, *, tq, tk, heads_per_step, row_chunk, lambda_init):
    qi = pl.program_id(2)
    hd = DA_HEAD_DIM
    vd = DA_VALUE_DIM
    lane_reps = tk // LANES
    heads = range(heads_per_step)

    zeros = jnp.zeros((tq, hd), qs_ref.dtype)
    for h in heads:
        qs_ref[h, 0:tq, 0:hd] = q_ref[:, h * vd:h * vd + hd]
        qs_ref[h, 0:tq, hd:] = zeros
        qs_ref[h, tq:, 0:hd] = zeros
        qs_ref[h, tq:, hd:] = q_ref[:, h * vd + hd:(h + 1) * vd]

    m_ref[...] = jnp.full(m_ref.shape, NEG_BIG, F32)
    l_ref[...] = jnp.zeros(l_ref.shape, F32)
    acc_ref[...] = jnp.zeros(acc_ref.shape, F32)

    def scores_to(s_ref, blk):
        start = pl.multiple_of(blk * tk, tk)
        for h in heads:
            s_ref[h] = lax.dot_general(qs_ref[h], k_ref[pl.ds(start, tk), h * vd:(h + 1) * vd],
                                       (((1,), (1,)), ((), ())), preferred_element_type=F32)

    def absorb(s_ref, blk, masked):
        start = pl.multiple_of(blk * tk, tk)
        for h in heads:
            for r0 in range(0, 2 * tq, row_chunk):
                rows = slice(r0, r0 + row_chunk)
                s = s_ref[h, rows, :]
                if masked:
                    q_pos = lax.broadcasted_iota(jnp.int32, (row_chunk, tk), 0) + (r0 % tq + qi * tq)
                    k_pos = lax.broadcasted_iota(jnp.int32, (row_chunk, tk), 1) + start
                    s = jnp.where(k_pos // CHUNK <= q_pos // CHUNK, s, NEG_BIG)
                m_prev = m_ref[h, rows, :]
                m_new = jnp.maximum(m_prev, jnp.max(s, axis=-1, keepdims=True))
                alpha = jnp.exp2(m_prev - m_new)
                p = jnp.exp2(s - jnp.tile(m_new, (1, lane_reps)))
                l_ref[h, rows, :] = alpha * l_ref[h, rows, :] + jnp.sum(p, axis=-1, keepdims=True)
                p_ref[h, rows, :] = p.astype(BF16)
                alpha_ref[h, rows, :] = alpha
                m_ref[h, rows, :] = m_new
            pv = jnp.dot(p_ref[h], v_ref[pl.ds(start, tk), h * vd:(h + 1) * vd], preferred_element_type=F32)
            acc_ref[h] = jnp.tile(alpha_ref[h], (1, vd // LANES)) * acc_ref[h] + pv

    n_full = (qi * tq) // tk
    n_pairs = n_full // 2
    scores_to(sa_ref, 0)

    def pair_step(jj, carry):
        j = 2 * jj
        absorb(sa_ref, j, False)
        scores_to(sb_ref, j + 1)
        absorb(sb_ref, j + 1, False)
        scores_to(sa_ref, j + 2)
        return carry

    lax.fori_loop(0, n_pairs, pair_step, 0)

    @pl.when(n_full % 2 == 0)
    def _():
        absorb(sa_ref, n_full, True)

    @pl.when(n_full % 2 == 1)
    def _():
        absorb(sa_ref, n_full - 1, False)
        scores_to(sb_ref, n_full)
        absorb(sb_ref, n_full, True)

    lp = lam_ref[...]
    lam = (jnp.exp(jnp.sum(lp[0:1] * lp[1:2], axis=-1, keepdims=True))
           - jnp.exp(jnp.sum(lp[2:3] * lp[3:4], axis=-1, keepdims=True)) + lambda_init)
    for h in heads:
        o = acc_ref[h] * jnp.tile(1.0 / l_ref[h], (1, vd // LANES))
        o = o[:tq] - lam * o[tq:]
        o = o * lax.rsqrt(jnp.mean(o * o, axis=-1, keepdims=True) + EPS)
        o = o * g_ref[...] * (1.0 - lambda_init)
        o_ref[:, h * vd:(h + 1) * vd] = o.astype(o_ref.dtype)


ATTN_Q_SCALE = DA_HEAD_DIM ** -0.5 * LOG2E


def diff_attention_core(qkv, lam_p, subln_g, *, batch, seq, lambda_init, tq=512, tk=512, heads_per_step=1,
                        row_chunk=64, name):
    m, d3 = qkv.shape
    d = d3 // 3
    heads = d // DA_VALUE_DIM
    tq = _fit(seq, tq)
    tk = _fit(seq, max(tk, tq))
    assert tk % tq == 0 and heads % heads_per_step == 0
    nq = seq // tq
    hg = heads // heads_per_step
    gw = heads_per_step * DA_VALUE_DIM
    vmem = pltpu.VMEM
    return pl.pallas_call(
        partial(_attn_kernel, tq=tq, tk=tk, heads_per_step=heads_per_step, row_chunk=row_chunk,
                lambda_init=lambda_init),
        out_shape=jax.ShapeDtypeStruct((m, d), BF16),
        grid=(batch, hg, nq),
        in_specs=[
            pl.BlockSpec((4, DA_HEAD_DIM), lambda b, h, i: (0, 0)),
            pl.BlockSpec((1, DA_VALUE_DIM), lambda b, h, i: (0, 0)),
            pl.BlockSpec((tq, gw), lambda b, h, i: (b * nq + i, h)),
            pl.BlockSpec((seq, gw), lambda b, h, i: (b, hg + h)),
            pl.BlockSpec((seq, gw), lambda b, h, i: (b, 2 * hg + h)),
        ],
        out_specs=pl.BlockSpec((tq, gw), lambda b, h, i: (b * nq + i, h)),
        scratch_shapes=[vmem((heads_per_step, 2 * tq, DA_VALUE_DIM), BF16),
                        vmem((heads_per_step, 2 * tq, LANES), F32), vmem((heads_per_step, 2 * tq, LANES), F32),
                        vmem((heads_per_step, 2 * tq, LANES), F32),
                        vmem((heads_per_step, 2 * tq, DA_VALUE_DIM), F32),
                        vmem((heads_per_step, 2 * tq, tk), BF16),
                        vmem((heads_per_step, 2 * tq, tk), F32), vmem((heads_per_step, 2 * tq, tk), F32)],
        compiler_params=_params(("parallel", "parallel", "arbitrary")),
        name=name,
    )(lam_p, subln_g.reshape(1, DA_VALUE_DIM), qkv, qkv, qkv)


def _retention_kernel(lg_ref, q_ref, k_ref, v_ref, g_ref, o_ref, r_ref, decay_ref, *, lc):
    h = pl.program_id(1)
    lg = lg_ref[h]

    @pl.when(pl.program_id(2) == 0)
    def _():
        r_ref[...] = jnp.zeros_like(r_ref)
        ni = lax.broadcasted_iota(jnp.int32, (lc, lc), 0)
        mi = lax.broadcasted_iota(jnp.int32, (lc, lc), 1)
        decay_ref[...] = jnp.where(mi // CHUNK <= ni // CHUNK,
                                   jnp.exp(lg * jnp.abs(ni - mi).astype(F32)), 0.0)

    q = q_ref[...]
    k = k_ref[...]
    v = v_ref[...]
    pos = lax.broadcasted_iota(jnp.int32, (lc, 1), 0).astype(F32)
    q_decay = jnp.exp(lg * (pos + 1.0))
    k_decay = jnp.exp(lg * (lc - 1.0 - pos))

    s = lax.dot_general(q, k, (((1,), (1,)), ((), ())), preferred_element_type=F32) * decay_ref[...]
    r = r_ref[...]
    o = jnp.dot(s.astype(BF16), v, preferred_element_type=F32)
    o = o + jnp.dot((q.astype(F32) * q_decay).astype(BF16), r.astype(BF16), preferred_element_type=F32)
    r_ref[...] = r * jnp.exp(lg * lc) + lax.dot_general(
        (k.astype(F32) * k_decay).astype(BF16), v, (((0,), (0,)), ((), ())), preferred_element_type=F32)

    o = o * lax.rsqrt(jnp.mean(o * o, axis=-1, keepdims=True) + EPS)
    o_ref[...] = (g_ref[...].astype(F32) * o).astype(o_ref.dtype)


def retention_core(qk, v, gate, *, batch, seq, lc=512, name):
    m, d2 = qk.shape
    heads = d2 // (2 * RET_QK_DIM)
    lc = _fit(seq, lc)
    nc = seq // lc
    log_gamma = jnp.log(1.0 - jnp.exp2(-5.0 - jnp.arange(heads, dtype=F32)))
    vmem = pltpu.VMEM
    return pl.pallas_call(
        partial(_retention_kernel, lc=lc),
        out_shape=jax.ShapeDtypeStruct((m, d2), BF16),
        grid=(batch, heads, nc),
        in_specs=[
            pl.BlockSpec(memory_space=pltpu.SMEM),
            pl.BlockSpec((lc, RET_QK_DIM), lambda b, h, c: (b * nc + c, h)),
            pl.BlockSpec((lc, RET_QK_DIM), lambda b, h, c: (b * nc + c, heads + h)),
            pl.BlockSpec((lc, RET_V_DIM), lambda b, h, c: (b * nc + c, h)),
            pl.BlockSpec((lc, RET_V_DIM), lambda b, h, c: (b * nc + c, h)),
        ],
        out_specs=pl.BlockSpec((lc, RET_V_DIM), lambda b, h, c: (b * nc + c, h)),
        scratch_shapes=[vmem((RET_QK_DIM, RET_V_DIM), F32), vmem((lc, lc), F32)],
        compiler_params=_params(("parallel", "parallel", "arbitrary")),
        name=name,
    )(log_gamma, qk, qk, v, gate)


def rope_tables(seq):
    half = RET_QK_DIM // 2
    inv = 1.0 / (ROPE_BASE ** jnp.linspace(0.0, 1.0, half, dtype=F32))
    ang = jnp.arange(seq, dtype=F32)[:, None] * inv[None, :]
    return jnp.cos(ang), jnp.sin(ang)


def _s5_norm_kernel(x_ref, g_ref, o_ref, slab_ref, asm_ref):
    tt = x_ref.shape[1]
    tiles = o_ref.shape[0]
    nk = tt // S5_SUB
    g = g_ref[...]
    for q in range(S5_SEQS):
        hn = _rmsnorm(x_ref[q], g)
        for j in range(tiles):
            slab_ref[...] = hn[:, j * LANES:(j + 1) * LANES]
            for s in range(S5_SUB):
                asm_ref[j, s, pl.ds(q, nk, stride=S5_SEQS), :] = slab_ref[pl.ds(s, nk, stride=S5_SUB), :]
    o_ref[...] = asm_ref[...].astype(o_ref.dtype)


def s5_norm_to_scan_layout(x, g, *, batch, seq, tt=64, name):
    m, d = x.shape
    tiles = d // LANES
    half = seq // 2
    tt = _fit(half, tt)
    return pl.pallas_call(
        _s5_norm_kernel,
        out_shape=jax.ShapeDtypeStruct((tiles, S5_SUB, m // S5_SUB, LANES), BF16),
        grid=(half // tt,),
        in_specs=[pl.BlockSpec((S5_SEQS, tt, d), lambda i: (0, i, 0)), pl.BlockSpec((1, d), lambda i: (0, 0))],
        out_specs=pl.BlockSpec((tiles, S5_SUB, tt, LANES), lambda i: (0, 0, i, 0)),
        scratch_shapes=[pltpu.VMEM((tt, LANES), F32), pltpu.VMEM((tiles, S5_SUB, tt, LANES), F32)],
        compiler_params=_params(("parallel",)),
        name=name,
    )(x.reshape(S5_SEQS, half, d), g.reshape(1, d))


def _s5_kernel(u_ref, kc_ref, wzc_ref, wyc_ref, a_ref, y_ref, ucat_ref, m_ref, wz_ref, wy_ref, z_ref, *,
               row_chunk):
    rows = u_ref.shape[2]
    ns = z_ref.shape[1] // 2
    n_sub = rows // S5_SEQS
    n_chunks = rows // row_chunk
    grp = LANES // S5_GROUP

    for s in range(S5_SUB):
        ucat_ref[:, s * LANES:(s + 1) * LANES] = u_ref[0, s]

    def same_group(shape, row_span, col_span):
        return (lax.broadcasted_iota(jnp.int32, shape, 0) // row_span
                == lax.broadcasted_iota(jnp.int32, shape, 1) // col_span)

    diag_k = same_group((LANES, LANES), S5_GROUP, S5_GROUP)
    zero_blk = jnp.zeros((LANES, LANES), m_ref.dtype)
    for tau in range(S5_SUB):
        blk = jnp.where(diag_k, jnp.tile(kc_ref[0, tau], (grp, 1)), 0.0).astype(m_ref.dtype)
        for s in range(S5_SUB - tau):
            t = s + tau
            m_ref[s * LANES:(s + 1) * LANES, t * LANES:(t + 1) * LANES] = blk
    for s in range(S5_SUB):
        for t in range(s):
            m_ref[s * LANES:(s + 1) * LANES, t * LANES:(t + 1) * LANES] = zero_blk
    diag_z = same_group((LANES, ns), S5_GROUP, S5_STATE)
    diag_y = same_group((ns, LANES), S5_STATE, S5_GROUP)
    for s in range(S5_SUB):
        for part in range(2):
            wz_ref[s * LANES:(s + 1) * LANES, part * ns:(part + 1) * ns] = jnp.where(
                diag_z, jnp.tile(wzc_ref[0, s, part], (grp, 1)), 0.0).astype(wz_ref.dtype)
            wy_ref[part * ns:(part + 1) * ns, s * LANES:(s + 1) * LANES] = jnp.where(
                diag_y, jnp.tile(wyc_ref[0, s, part], (grp, 1)), 0.0).astype(wy_ref.dtype)

    def z_body(i, carry):
        rs = pl.ds(pl.multiple_of(i * row_chunk, row_chunk), row_chunk)
        z_ref[rs, :] = jnp.dot(ucat_ref[rs, :], wz_ref[...], preferred_element_type=F32)
        return carry

    lax.fori_loop(0, n_chunks, z_body, 0)

    a_re = jnp.broadcast_to(a_ref[0, 0:1, :], (S5_SEQS, ns))
    a_im = jnp.broadcast_to(a_ref[0, 1:2, :], (S5_SEQS, ns))

    def advance(k, sr, si):
        rs = pl.ds(pl.multiple_of(k * S5_SEQS, S5_SEQS), S5_SEQS)
        zr = z_ref[rs, :ns]
        zi = z_ref[rs, ns:]
        return rs, a_re * sr - a_im * si + zr, a_re * si + a_im * sr + zi

    def pass1(k, st):
        _, nr, ni = advance(k, *st)
        return nr, ni

    zero = jnp.zeros((S5_SEQS, ns), F32)
    end_re, end_im = lax.fori_loop(0, n_sub, pass1, (zero, zero))

    odd = lax.broadcasted_iota(jnp.int32, (S5_SEQS, ns), 0) % 2 == 1
    init = (jnp.where(odd, pltpu.roll(end_re, 1, axis=0), 0.0),
            jnp.where(odd, pltpu.roll(end_im, 1, axis=0), 0.0))

    def pass2(k, st):
        rs, nr, ni = advance(k, *st)
        z_ref[rs, :ns] = st[0]
        z_ref[rs, ns:] = st[1]
        return nr, ni

    lax.fori_loop(0, n_sub, pass2, init)

    def y_body(i, carry):
        rs = pl.ds(pl.multiple_of(i * row_chunk, row_chunk), row_chunk)
        y = (jnp.dot(ucat_ref[rs, :], m_ref[...], preferred_element_type=F32)
             + jnp.dot(z_ref[rs, :].astype(BF16), wy_ref[...], preferred_element_type=F32))
        for s in range(S5_SUB):
            y_ref[0, s, rs, :] = y[:, s * LANES:(s + 1) * LANES]
        return carry

    lax.fori_loop(0, n_chunks, y_body, 0)


def s5_core(u, kc, wzc, wyc, a_sub, *, row_chunk=512, name):
    tiles, sub, rows, _ = u.shape
    ns = a_sub.shape[2]
    lk = sub * LANES
    row_chunk = _fit(rows, row_chunk)
    return pl.pallas_call(
        partial(_s5_kernel, row_chunk=row_chunk),
        out_shape=jax.ShapeDtypeStruct((tiles, sub, rows, LANES), F32),
        grid=(tiles,),
        in_specs=[
            pl.BlockSpec((1, sub, rows, LANES), lambda j: (j, 0, 0, 0)),
            pl.BlockSpec((1, sub, S5_GROUP, LANES), lambda j: (j, 0, 0, 0)),
            pl.BlockSpec((1, sub, 2, S5_GROUP, ns), lambda j: (j, 0, 0, 0, 0)),
            pl.BlockSpec((1, sub, 2, S5_STATE, LANES), lambda j: (j, 0, 0, 0, 0)),
            pl.BlockSpec((1, 2, ns), lambda j: (j, 0, 0)),
        ],
        out_specs=pl.BlockSpec((1, sub, rows, LANES), lambda j: (j, 0, 0, 0)),
        scratch_shapes=[pltpu.VMEM((rows, lk), BF16), pltpu.VMEM((lk, lk), BF16), pltpu.VMEM((lk, 2 * ns), BF16),
                        pltpu.VMEM((2 * ns, lk), BF16), pltpu.VMEM((rows, 2 * ns), F32)],
        compiler_params=_params(("parallel",)),
        name=name,
    )(u, kc, wzc, wyc, a_sub)


def _s5_operators(a_re, a_im, log_dt, b_re, b_im, c_re, c_im):
    g, p = a_re.shape
    c = S5_GROUP
    gpt = LANES // c
    tiles = g // gpt
    ls = S5_SUB

    lam_re, lam_im = a_re.astype(F32), a_im.astype(F32)
    dt = jnp.exp(log_dt.astype(F32))[:, None]
    mag = jnp.exp(lam_re * dt)
    ab_re = mag * jnp.cos(lam_im * dt)
    ab_im = mag * jnp.sin(lam_im * dt)
    den = lam_re * lam_re + lam_im * lam_im
    nr, ni = ab_re - 1.0, ab_im
    coef_re = (nr * lam_re + ni * lam_im) / den
    coef_im = (ni * lam_re - nr * lam_im) / den
    br, bi = b_re.astype(F32), b_im.astype(F32)
    bb_re = coef_re[..., None] * br - coef_im[..., None] * bi
    bb_im = coef_re[..., None] * bi + coef_im[..., None] * br
    cr, ci = c_re.astype(F32), c_im.astype(F32)

    pw_re, pw_im = [jnp.ones_like(ab_re)], [jnp.zeros_like(ab_im)]
    for _ in range(ls):
        r0, i0 = pw_re[-1], pw_im[-1]
        pw_re.append(r0 * ab_re - i0 * ab_im)
        pw_im.append(r0 * ab_im + i0 * ab_re)
    pw_re, pw_im = jnp.stack(pw_re), jnp.stack(pw_im)

    def per_tile(t, lead):
        r, w = t.shape[-2:]
        n = len(lead)
        t = t.reshape(*lead, tiles, gpt, r, w)
        t = jnp.transpose(t, (n, *range(n), n + 2, n + 1, n + 3))
        return t.reshape(tiles, *lead, r, gpt * w)

    cb_re = jnp.einsum('gop,tgp->tgop', cr, pw_re[:ls]) - jnp.einsum('gop,tgp->tgop', ci, pw_im[:ls])
    cb_im = jnp.einsum('gop,tgp->tgop', cr, pw_im[:ls]) + jnp.einsum('gop,tgp->tgop', ci, pw_re[:ls])
    bbt_re, bbt_im = bb_re.transpose(0, 2, 1)[None, :, :, None, :], bb_im.transpose(0, 2, 1)[None, :, :, None, :]
    kern = jnp.sum(cb_re[:, :, None, :, :] * bbt_re - cb_im[:, :, None, :, :] * bbt_im, axis=-1)
    kc = per_tile(kern, (ls,))

    rev_re, rev_im = pw_re[ls - 1::-1], pw_im[ls - 1::-1]
    wz_re = jnp.einsum('sgp,gpi->sgip', rev_re, bb_re) - jnp.einsum('sgp,gpi->sgip', rev_im, bb_im)
    wz_im = jnp.einsum('sgp,gpi->sgip', rev_re, bb_im) + jnp.einsum('sgp,gpi->sgip', rev_im, bb_re)
    wzc = per_tile(jnp.stack([wz_re, wz_im], axis=1), (ls, 2))

    wy_re = jnp.einsum('gop,tgp->tgpo', cr, pw_re[1:]) - jnp.einsum('gop,tgp->tgpo', ci, pw_im[1:])
    wy_im = -(jnp.einsum('gop,tgp->tgpo', cr, pw_im[1:]) + jnp.einsum('gop,tgp->tgpo', ci, pw_re[1:]))
    wyc = per_tile(jnp.stack([wy_re, wy_im], axis=1), (ls, 2))

    a_sub = jnp.stack([pw_re[ls].reshape(tiles, gpt * p), pw_im[ls].reshape(tiles, gpt * p)], axis=1)
    return kc, wzc, wyc, a_sub


def _s5_act_kernel(x_ref, y_ref, g_ref, d_ref, o_ref, slab_ref):
    tt = x_ref.shape[1]
    tiles = y_ref.shape[0]
    nk = tt // S5_SUB
    for q in range(S5_SEQS):
        x = x_ref[q]
        inv_rms = lax.rsqrt(jnp.mean(x * x, axis=-1, keepdims=True) + EPS)
        for j in range(tiles):
            lanes = slice(j * LANES, (j + 1) * LANES)
            for s in range(S5_SUB):
                slab_ref[pl.ds(s, nk, stride=S5_SUB), :] = y_ref[j, s, pl.ds(q, nk, stride=S5_SEQS), :]
            hn = x_ref[q, :, lanes] * inv_rms * g_ref[:, lanes]
            o_ref[q, :, lanes] = jax.nn.gelu(slab_ref[...] + d_ref[:, lanes] * hn).astype(o_ref.dtype)


def s5_activation(x, y_scan, g, d_skip, *, batch, seq, tt=64, name):
    m, d = x.shape
    tiles = d // LANES
    half = seq // 2
    tt = _fit(half, tt)
    act = pl.pallas_call(
        _s5_act_kernel,
        out_shape=jax.ShapeDtypeStruct((S5_SEQS, half, d), BF16),
        grid=(half // tt,),
        in_specs=[
            pl.BlockSpec((S5_SEQS, tt, d), lambda i: (0, i, 0)),
            pl.BlockSpec((tiles, S5_SUB, tt, LANES), lambda i: (0, 0, i, 0)),
            pl.BlockSpec((1, d), lambda i: (0, 0)),
            pl.BlockSpec((1, d), lambda i: (0, 0)),
        ],
        out_specs=pl.BlockSpec((S5_SEQS, tt, d), lambda i: (0, i, 0)),
        scratch_shapes=[pltpu.VMEM((tt, LANES), F32)],
        compiler_params=_params(("parallel",)),
        name=name,
    )(x.reshape(S5_SEQS, half, d), y_scan, g.reshape(1, d), d_skip.reshape(1, d))
    return act.reshape(m, d)


def _glu_residual_kernel(a_ref, wv_ref, wg_ref, x_ref, o_ref):
    a = a_ref[...]
    val = jnp.dot(a, wv_ref[...], preferred_element_type=F32)
    gate = jnp.dot(a, wg_ref[...], preferred_element_type=F32)
    o_ref[...] = x_ref[...] + val * jax.nn.sigmoid(gate)


def glu_residual(a, w_glu, layer, x, *, tm=1024, tn=512, name):
    m, d = x.shape
    tm, tn = _fit(m, tm), _fit(d, tn)
    nb = d // tn
    return pl.pallas_call(
        _glu_residual_kernel,
        out_shape=jax.ShapeDtypeStruct((m, d), F32),
        grid=(m // tm, nb),
        in_specs=[
            pl.BlockSpec((tm, d), lambda i, j: (i, 0)),
            pl.BlockSpec((None, d, tn), lambda i, j: (layer, 0, j)),
            pl.BlockSpec((None, d, tn), lambda i, j: (layer, 0, nb + j)),
            pl.BlockSpec((tm, tn), lambda i, j: (i, j)),
        ],
        out_specs=pl.BlockSpec((tm, tn), lambda i, j: (i, j)),
        compiler_params=_params(("parallel", "arbitrary")),
        name=name,
    )(a, w_glu, w_glu, x)


def s5_mixer_residual(x, g, a_re, a_im, log_dt, b_re, b_im, c_re, c_im, d_skip, w_glu, layer, *, batch, seq,
                      tag):
    u = s5_norm_to_scan_layout(x, g, batch=batch, seq=seq, name=f"s5_norm_{tag}")
    kc, wzc, wyc, a_sub = _s5_operators(a_re, a_im, log_dt, b_re, b_im, c_re, c_im)
    y = s5_core(u, kc, wzc, wyc, a_sub, name=f"s5_scan_{tag}")
    act = s5_activation(x, y, g, d_skip, batch=batch, seq=seq, name=f"s5_act_{tag}")
    return glu_residual(act, w_glu, layer, x, name=f"s5_glu_{tag}")


def kernel(x, norm_mix, norm_mlp, norm_final, a_w_in, a_lambda, a_subln, a_w_out, b_a_re, b_a_im, b_log_dt, b_b_re, b_b_im, b_c_re, b_c_im, b_d, b_w_glu, c_w_in, c_w_out, mlp_w1, mlp_w2):
    batch, seq, d = x.shape
    depth = norm_mix.shape[0]
    assert batch * 2 == S5_SEQS, "S5 scan layout places batch x two time halves on the 8 sublanes"
    xf = x.reshape(batch * seq, d)
    a_w_in, a_w_out, b_w_glu, c_w_in, c_w_out, mlp_w1, mlp_w2 = (
        w.astype(BF16) for w in (a_w_in, a_w_out, b_w_glu, c_w_in, c_w_out, mlp_w1, mlp_w2))

    for i in range(depth):
        kind = i % N_MIXERS
        j = i // N_MIXERS
        if kind == 0:
            lambda_init = 0.8 - 0.6 * math.exp(-0.3 * i)
            qkv = norm_matmul(xf, norm_mix[i], a_w_in, j, scaled_cols=d, scale=ATTN_Q_SCALE, name=f"attn_in_{i}")
            o = diff_attention_core(qkv, a_lambda[j], a_subln[j], batch=batch, seq=seq,
                                    lambda_init=lambda_init, name=f"attn_core_{i}")
            xf = matmul_residual(o, a_w_out, j, xf, tm=512, tn=d, name=f"attn_out_{i}")
        elif kind == 1:
            xf = s5_mixer_residual(xf, norm_mix[i], b_a_re[j], b_a_im[j], b_log_dt[j], b_b_re[j],
                                   b_b_im[j], b_c_re[j], b_c_im[j], b_d[j], b_w_glu, j,
                                   batch=batch, seq=seq, tag=str(i))
        else:
            qk = norm_matmul(xf, norm_mix[i], c_w_in, j, col_start=0, n_cols=2 * d, epilogue="rope",
                             scaled_cols=d, scale=RET_QK_DIM ** -0.5, rope=rope_tables(seq), seq=seq,
                             head_dim=RET_QK_DIM, name=f"ret_in_qk_{i}")
            v = norm_matmul(xf, norm_mix[i], c_w_in, j, col_start=2 * d, n_cols=2 * d, name=f"ret_in_v_{i}")
            gate = norm_matmul(xf, norm_mix[i], c_w_in, j, col_start=4 * d, n_cols=2 * d, epilogue="silu",
                               name=f"ret_in_g_{i}")
            o = retention_core(qk, v, gate, batch=batch, seq=seq, name=f"ret_core_{i}")
            xf = matmul_residual(o, c_w_out, j, xf, name=f"ret_out_{i}")
        xf = mlp_residual(xf, norm_mlp[i], mlp_w1, mlp_w2, i, norm_final,
                          final_norm=(i == depth - 1), name=f"mlp_{i}")
    return xf.reshape(batch, seq, d)
```

```python
import math
from functools import partial

import jax
import jax.numpy as jnp
from jax import lax
from jax.experimental import pallas as pl
from jax.experimental.pallas import tpu as pltpu

F32 = jnp.float32
BF16 = jnp.bfloat16

EPS = 1e-6
CHUNK = 64
N_MIXERS = 3
DA_HEAD_DIM = 128
DA_VALUE_DIM = 2 * DA_HEAD_DIM
RET_QK_DIM = 256
RET_V_DIM = 2 * RET_QK_DIM
ROPE_BASE = 10000.0
S5_GROUP = 16
S5_STATE = 64

LANES = 128
SUBLANES = 8
VMEM_LIMIT_BYTES = 56 * 1024 * 1024
LOG2E = math.log2(math.e)
NEG_BIG = -1e30

NORM_ROWS = 256
S5_SUB = 8
S5_SEQS = SUBLANES


def _params(semantics):
    return pltpu.CompilerParams(dimension_semantics=semantics, vmem_limit_bytes=VMEM_LIMIT_BYTES)


def _fit(n, preferred):
    t = min(preferred, n)
    while n % t:
        t //= 2
    return t


def _rmsnorm(x, g):
    y = x * lax.rsqrt(jnp.mean(x * x, axis=-1, keepdims=True) + EPS)
    return y * g


def _norm_block_to(x_ref, g_ref, dst_ref):
    rows_total = x_ref.shape[0]
    step = min(NORM_ROWS, rows_total)
    g = g_ref[...]

    def body(r, carry):
        rows = pl.ds(pl.multiple_of(r * step, step), step)
        dst_ref[rows, :] = _rmsnorm(x_ref[rows, :], g).astype(dst_ref.dtype)
        return carry

    lax.fori_loop(0, rows_total // step, body, 0)


def _norm_matmul_kernel(x_ref, g_ref, w_ref, *rest, epilogue, scaled_blocks, scale, head_dim):
    if epilogue == "rope":
        cos_ref, sin_ref, o_ref, hn_ref = rest
    else:
        o_ref, hn_ref = rest
    j = pl.program_id(1)

    @pl.when(j == 0)
    def _():
        _norm_block_to(x_ref, g_ref, hn_ref)

    acc = jnp.dot(hn_ref[...], w_ref[...], preferred_element_type=F32)
    if epilogue == "scale":
        if scaled_blocks:
            acc = acc * jnp.where(j < scaled_blocks, scale, 1.0)
        o_ref[...] = acc.astype(o_ref.dtype)
    elif epilogue == "silu":
        o_ref[...] = (acc * jax.nn.sigmoid(acc)).astype(o_ref.dtype)
    else:
        half = head_dim // 2
        cos = cos_ref[...]
        sin = sin_ref[...]
        sc = jnp.where(j < scaled_blocks, 1.0, scale)
        for c0 in range(0, acc.shape[1], head_dim):
            t1 = acc[:, c0:c0 + half]
            t2 = acc[:, c0 + half:c0 + head_dim]
            o_ref[:, c0:c0 + half] = ((t1 * cos - t2 * sin) * sc).astype(o_ref.dtype)
            o_ref[:, c0 + half:c0 + head_dim] = ((t1 * sin + t2 * cos) * sc).astype(o_ref.dtype)


def norm_matmul(x, g, w, layer, *, col_start=0, n_cols=None, epilogue="scale", scaled_cols=0, scale=1.0,
                rope=None, seq=None, head_dim=None, tm=1024, tn=2048, name):
    m, d = x.shape
    n = w.shape[2] - col_start if n_cols is None else n_cols
    tm, tn = _fit(m, tm), _fit(math.gcd(n, scaled_cols, col_start), tn)
    jb = col_start // tn
    in_specs = [
        pl.BlockSpec((tm, d), lambda i, j: (i, 0)),
        pl.BlockSpec((1, d), lambda i, j: (0, 0)),
        pl.BlockSpec((None, d, tn), lambda i, j: (layer, 0, jb + j)),
    ]
    args = [x, g.reshape(1, d), w]
    if epilogue == "rope":
        assert seq % tm == 0 and tn % head_dim == 0
        pos_blocks = seq // tm
        in_specs += [pl.BlockSpec((tm, head_dim // 2), lambda i, j: (i % pos_blocks, 0))] * 2
        args += list(rope)
    return pl.pallas_call(
        partial(_norm_matmul_kernel, epilogue=epilogue, scaled_blocks=scaled_cols // tn, scale=scale,
                head_dim=head_dim),
        out_shape=jax.ShapeDtypeStruct((m, n), BF16),
        grid=(m // tm, n // tn),
        in_specs=in_specs,
        out_specs=pl.BlockSpec((tm, tn), lambda i, j: (i, j)),
        scratch_shapes=[pltpu.VMEM((tm, d), BF16)],
        compiler_params=_params(("parallel", "arbitrary")),
        name=name,
    )(*args)


def _matmul_residual_kernel(a_ref, w_ref, x_ref, o_ref):
    o_ref[...] = x_ref[...] + jnp.dot(a_ref[...], w_ref[...], preferred_element_type=F32)


def matmul_residual(a, w, layer, x, *, tm=1024, tn=512, name):
    m, k = a.shape
    n = w.shape[2]
    tm, tn = _fit(m, tm), _fit(n, tn)
    return pl.pallas_call(
        _matmul_residual_kernel,
        out_shape=jax.ShapeDtypeStruct((m, n), F32),
        grid=(m // tm, n // tn),
        in_specs=[
            pl.BlockSpec((tm, k), lambda i, j: (i, 0)),
            pl.BlockSpec((None, k, tn), lambda i, j: (layer, 0, j)),
            pl.BlockSpec((tm, tn), lambda i, j: (i, j)),
        ],
        out_specs=pl.BlockSpec((tm, tn), lambda i, j: (i, j)),
        compiler_params=_params(("parallel", "arbitrary")),
        name=name,
    )(a, w, x)


def _mlp_kernel(x_ref, g_ref, w1_ref, w2_ref, gf_ref, o_ref, hn_ref, *, final_norm):
    f = pl.program_id(1)

    @pl.when(f == 0)
    def _():
        _norm_block_to(x_ref, g_ref, hn_ref)
        o_ref[...] = x_ref[...]

    a = jnp.dot(hn_ref[...], w1_ref[...], preferred_element_type=F32)
    a = jnp.maximum(a, 0.0)
    a = (a * a).astype(BF16)
    o_ref[...] += jnp.dot(a, w2_ref[...], preferred_element_type=F32)

    if final_norm:
        @pl.when(f == pl.num_programs(1) - 1)
        def _():
            _norm_block_to(o_ref, gf_ref, o_ref)


def mlp_residual(x, g, w1, w2, layer, g_final, *, final_norm, tm=512, tf=1024, name):
    m, d = x.shape
    ff = w1.shape[2]
    tm, tf = _fit(m, tm), _fit(ff, tf)
    return pl.pallas_call(
        partial(_mlp_kernel, final_norm=final_norm),
        out_shape=jax.ShapeDtypeStruct((m, d), F32),
        grid=(m // tm, ff // tf),
        in_specs=[
            pl.BlockSpec((tm, d), lambda i, f: (i, 0)),
            pl.BlockSpec((1, d), lambda i, f: (0, 0)),
            pl.BlockSpec((None, d, tf), lambda i, f: (layer, 0, f)),
            pl.BlockSpec((None, tf, d), lambda i, f: (layer, f, 0)),
            pl.BlockSpec((1, d), lambda i, f: (0, 0)),
        ],
        out_specs=pl.BlockSpec((tm, d), lambda i, f: (i, 0)),
        scratch_shapes=[pltpu.VMEM((tm, d), BF16)],
        compiler_params=_params(("parallel", "arbitrary")),
        name=name,
    )(x, g.reshape(1, d), w1, w2, g_final.reshape(1, d))


ATTN_Q_GROUP = 4


def _attn_kernel(lam_ref, g_ref, q_ref, k_ref, v_ref, o_ref, qs_ref, m_ref, l_ref, alpha_ref, acc_ref, p_ref,
                 sa_ref, sb_ref, *, tq, row_chunk, lambda_init):
    seq = k_ref.shape[0]
    nq = seq // tq
    tk = tq
    hd = DA_HEAD_DIM
    vd = DA_VALUE_DIM
    lane_reps = tk // LANES
    bufs = (sa_ref, sb_ref)

    lp = lam_ref[...]
    lam = (jnp.exp(jnp.sum(lp[0:1] * lp[1:2], axis=-1, keepdims=True))
           - jnp.exp(jnp.sum(lp[2:3] * lp[3:4], axis=-1, keepdims=True)) + lambda_init)

    def load_queries(slot, qi):
        rows = pl.ds(pl.multiple_of(qi * tq, tq), tq)
        zeros = jnp.zeros((tq, hd), qs_ref.dtype)
        qs_ref[slot, 0:tq, 0:hd] = q_ref[rows, 0:hd]
        qs_ref[slot, 0:tq, hd:] = zeros
        qs_ref[slot, tq:, 0:hd] = zeros
        qs_ref[slot, tq:, hd:] = q_ref[rows, hd:]

    def scores_to(s_ref, slot, blk):
        start = pl.multiple_of(blk * tk, tk)
        s_ref[...] = lax.dot_general(qs_ref[slot], k_ref[pl.ds(start, tk), :], (((1,), (1,)), ((), ())),
                                     preferred_element_type=F32)

    def absorb(s_ref, qi, blk, masked):
        start = pl.multiple_of(blk * tk, tk)
        for r0 in range(0, 2 * tq, row_chunk):
            rows = slice(r0, r0 + row_chunk)
            s = s_ref[rows, :]
            if masked:
                q_pos = lax.broadcasted_iota(jnp.int32, (row_chunk, tk), 0) + (r0 % tq + qi * tq)
                k_pos = lax.broadcasted_iota(jnp.int32, (row_chunk, tk), 1) + start
                s = jnp.where(k_pos // CHUNK <= q_pos // CHUNK, s, NEG_BIG)
            m_prev = m_ref[rows, :]
            m_new = jnp.maximum(m_prev, jnp.max(s, axis=-1, keepdims=True))
            alpha = jnp.exp2(m_prev - m_new)
            p = jnp.exp2(s - jnp.tile(m_new, (1, lane_reps)))
            l_ref[rows, :] = alpha * l_ref[rows, :] + jnp.sum(p, axis=-1, keepdims=True)
            p_ref[rows, :] = p.astype(BF16)
            alpha_ref[rows, :] = alpha
            m_ref[rows, :] = m_new
        pv = jnp.dot(p_ref[...], v_ref[pl.ds(start, tk), :], preferred_element_type=F32)
        acc_ref[...] = jnp.tile(alpha_ref[...], (1, vd // LANES)) * acc_ref[...] + pv

    def finish(qi):
        o = acc_ref[...] * jnp.tile(1.0 / l_ref[...], (1, vd // LANES))
        o = o[:tq] - lam * o[tq:]
        o = o * lax.rsqrt(jnp.mean(o * o, axis=-1, keepdims=True) + EPS)
        o = o * g_ref[...] * (1.0 - lambda_init)
        o_ref[pl.ds(pl.multiple_of(qi * tq, tq), tq), :] = o.astype(o_ref.dtype)

    def query_block(qi, offset, first):
        x, y = bufs[first], bufs[1 - first]
        slot = offset % 2
        m_ref[...] = jnp.full(m_ref.shape, NEG_BIG, F32)
        l_ref[...] = jnp.zeros(l_ref.shape, F32)
        acc_ref[...] = jnp.zeros(acc_ref.shape, F32)

        def pair_step(jj, carry):
            j = 2 * jj
            absorb(x, qi, j, False)
            scores_to(y, slot, j + 1)
            absorb(y, qi, j + 1, False)
            scores_to(x, slot, j + 2)
            return carry

        lax.fori_loop(0, qi // 2, pair_step, 0)
        nxt = jnp.minimum(qi + 1, nq - 1)
        if offset % 2 == 0:
            load_queries(1 - slot, nxt)
            absorb(x, qi, qi, True)
            scores_to(y, 1 - slot, 0)
        else:
            absorb(x, qi, qi - 1, False)
            scores_to(y, slot, qi)
            load_queries(1 - slot, nxt)
            absorb(y, qi, qi, True)
            scores_to(x, 1 - slot, 0)
        finish(qi)

    load_queries(0, 0)
    scores_to(sa_ref, 0, 0)

    def group_step(a, carry):
        base = ATTN_Q_GROUP * a
        query_block(base, 0, 0)
        query_block(base + 1, 1, 1)
        query_block(base + 2, 2, 1)
        query_block(base + 3, 3, 0)
        return carry

    lax.fori_loop(0, nq // ATTN_Q_GROUP, group_step, 0)


ATTN_Q_SCALE = DA_HEAD_DIM ** -0.5 * LOG2E


def diff_attention_core(qkv, lam_p, subln_g, *, batch, seq, lambda_init, tq=512, row_chunk=64, name):
    m, d3 = qkv.shape
    d = d3 // 3
    heads = d // DA_VALUE_DIM
    tq = _fit(seq // ATTN_Q_GROUP, tq)
    assert seq % (ATTN_Q_GROUP * tq) == 0 and tq % CHUNK == 0 and tq % row_chunk == 0
    vmem = pltpu.VMEM
    return pl.pallas_call(
        partial(_attn_kernel, tq=tq, row_chunk=row_chunk, lambda_init=lambda_init),
        out_shape=jax.ShapeDtypeStruct((m, d), BF16),
        grid=(batch, heads),
        in_specs=[
            pl.BlockSpec((4, DA_HEAD_DIM), lambda b, h: (0, 0)),
            pl.BlockSpec((1, DA_VALUE_DIM), lambda b, h: (0, 0)),
            pl.BlockSpec((seq, DA_VALUE_DIM), lambda b, h: (b, h)),
            pl.BlockSpec((seq, DA_VALUE_DIM), lambda b, h: (b, heads + h)),
            pl.BlockSpec((seq, DA_VALUE_DIM), lambda b, h: (b, 2 * heads + h)),
        ],
        out_specs=pl.BlockSpec((seq, DA_VALUE_DIM), lambda b, h: (b, h)),
        scratch_shapes=[vmem((2, 2 * tq, DA_VALUE_DIM), BF16),
                        vmem((2 * tq, LANES), F32), vmem((2 * tq, LANES), F32), vmem((2 * tq, LANES), F32),
                        vmem((2 * tq, DA_VALUE_DIM), F32), vmem((2 * tq, tq), BF16),
                        vmem((2 * tq, tq), F32), vmem((2 * tq, tq), F32)],
        compiler_params=_params(("parallel", "parallel")),
        name=name,
    )(lam_p, subln_g.reshape(1, DA_VALUE_DIM), qkv, qkv, qkv)


def _retention_kernel(lg_ref, q_ref, k_ref, v_ref, g_ref, o_ref, r_ref, decay_ref, *, lc):
    h = pl.program_id(1)
    lg = lg_ref[h]

    @pl.when(pl.program_id(2) == 0)
    def _():
        r_ref[...] = jnp.zeros_like(r_ref)
        ni = lax.broadcasted_iota(jnp.int32, (lc, lc), 0)
        mi = lax.broadcasted_iota(jnp.int32, (lc, lc), 1)
        decay_ref[...] = jnp.where(mi // CHUNK <= ni // CHUNK,
                                   jnp.exp(lg * jnp.abs(ni - mi).astype(F32)), 0.0)

    q = q_ref[...]
    k = k_ref[...]
    v = v_ref[...]
    pos = lax.broadcasted_iota(jnp.int32, (lc, 1), 0).astype(F32)
    q_decay = jnp.exp(lg * (pos + 1.0))
    k_decay = jnp.exp(lg * (lc - 1.0 - pos))

    s = lax.dot_general(q, k, (((1,), (1,)), ((), ())), preferred_element_type=F32) * decay_ref[...]
    r = r_ref[...]
    o = jnp.dot(s.astype(BF16), v, preferred_element_type=F32)
    o = o + jnp.dot((q.astype(F32) * q_decay).astype(BF16), r.astype(BF16), preferred_element_type=F32)
    r_ref[...] = r * jnp.exp(lg * lc) + lax.dot_general(
        (k.astype(F32) * k_decay).astype(BF16), v, (((0,), (0,)), ((), ())), preferred_element_type=F32)

    o = o * lax.rsqrt(jnp.mean(o * o, axis=-1, keepdims=True) + EPS)
    o_ref[...] = (g_ref[...].astype(F32) * o).astype(o_ref.dtype)


def retention_core(qk, v, gate, *, batch, seq, lc=512, name):
    m, d2 = qk.shape
    heads = d2 // (2 * RET_QK_DIM)
    lc = _fit(seq, lc)
    nc = seq // lc
    log_gamma = jnp.log(1.0 - jnp.exp2(-5.0 - jnp.arange(heads, dtype=F32)))
    vmem = pltpu.VMEM
    return pl.pallas_call(
        partial(_retention_kernel, lc=lc),
        out_shape=jax.ShapeDtypeStruct((m, d2), BF16),
        grid=(batch, heads, nc),
        in_specs=[
            pl.BlockSpec(memory_space=pltpu.SMEM),
            pl.BlockSpec((lc, RET_QK_DIM), lambda b, h, c: (b * nc + c, h)),
            pl.BlockSpec((lc, RET_QK_DIM), lambda b, h, c: (b * nc + c, heads + h)),
            pl.BlockSpec((lc, RET_V_DIM), lambda b, h, c: (b * nc + c, h)),
            pl.BlockSpec((lc, RET_V_DIM), lambda b, h, c: (b * nc + c, h)),
        ],
        out_specs=pl.BlockSpec((lc, RET_V_DIM), lambda b, h, c: (b * nc + c, h)),
        scratch_shapes=[vmem((RET_QK_DIM, RET_V_DIM), F32), vmem((lc, lc), F32)],
        compiler_params=_params(("parallel", "parallel", "arbitrary")),
        name=name,
    )(log_gamma, qk, qk, v, gate)


def rope_tables(seq):
    half = RET_QK_DIM // 2
    inv = 1.0 / (ROPE_BASE ** jnp.linspace(0.0, 1.0, half, dtype=F32))
    ang = jnp.arange(seq, dtype=F32)[:, None] * inv[None, :]
    return jnp.cos(ang), jnp.sin(ang)


def _s5_norm_kernel(x_ref, g_ref, o_ref, slab_ref, asm_ref):
    tt = x_ref.shape[1]
    tiles = o_ref.shape[0]
    nk = tt // S5_SUB
    g = g_ref[...]
    for q in range(S5_SEQS):
        hn = _rmsnorm(x_ref[q], g)
        for j in range(tiles):
            slab_ref[...] = hn[:, j * LANES:(j + 1) * LANES]
            for s in range(S5_SUB):
                asm_ref[j, s, pl.ds(q, nk, stride=S5_SEQS), :] = slab_ref[pl.ds(s, nk, stride=S5_SUB), :]
    o_ref[...] = asm_ref[...].astype(o_ref.dtype)


def s5_norm_to_scan_layout(x, g, *, batch, seq, tt=64, name):
    m, d = x.shape
    tiles = d // LANES
    half = seq // 2
    tt = _fit(half, tt)
    return pl.pallas_call(
        _s5_norm_kernel,
        out_shape=jax.ShapeDtypeStruct((tiles, S5_SUB, m // S5_SUB, LANES), BF16),
        grid=(half // tt,),
        in_specs=[pl.BlockSpec((S5_SEQS, tt, d), lambda i: (0, i, 0)), pl.BlockSpec((1, d), lambda i: (0, 0))],
        out_specs=pl.BlockSpec((tiles, S5_SUB, tt, LANES), lambda i: (0, 0, i, 0)),
        scratch_shapes=[pltpu.VMEM((tt, LANES), F32), pltpu.VMEM((tiles, S5_SUB, tt, LANES), F32)],
        compiler_params=_params(("parallel",)),
        name=name,
    )(x.reshape(S5_SEQS, half, d), g.reshape(1, d))


def _s5_kernel(u_ref, kc_ref, wzc_ref, wyc_ref, a_ref, y_ref, ucat_ref, m_ref, wz_ref, wy_ref, z_ref, *,
               row_chunk):
    rows = u_ref.shape[2]
    ns = z_ref.shape[1] // 2
    n_sub = rows // S5_SEQS
    n_chunks = rows // row_chunk
    grp = LANES // S5_GROUP

    for s in range(S5_SUB):
        ucat_ref[:, s * LANES:(s + 1) * LANES] = u_ref[0, s]

    def same_group(shape, row_span, col_span):
        return (lax.broadcasted_iota(jnp.int32, shape, 0) // row_span
                == lax.broadcasted_iota(jnp.int32, shape, 1) // col_span)

    diag_k = same_group((LANES, LANES), S5_GROUP, S5_GROUP)
    zero_blk = jnp.zeros((LANES, LANES), m_ref.dtype)
    for tau in range(S5_SUB):
        blk = jnp.where(diag_k, jnp.tile(kc_ref[0, tau], (grp, 1)), 0.0).astype(m_ref.dtype)
        for s in range(S5_SUB - tau):
            t = s + tau
            m_ref[s * LANES:(s + 1) * LANES, t * LANES:(t + 1) * LANES] = blk
    for s in range(S5_SUB):
        for t in range(s):
            m_ref[s * LANES:(s + 1) * LANES, t * LANES:(t + 1) * LANES] = zero_blk
    diag_z = same_group((LANES, ns), S5_GROUP, S5_STATE)
    diag_y = same_group((ns, LANES), S5_STATE, S5_GROUP)
    for s in range(S5_SUB):
        for part in range(2):
            wz_ref[s * LANES:(s + 1) * LANES, part * ns:(part + 1) * ns] = jnp.where(
                diag_z, jnp.tile(wzc_ref[0, s, part], (grp, 1)), 0.0).astype(wz_ref.dtype)
            wy_ref[part * ns:(part + 1) * ns, s * LANES:(s + 1) * LANES] = jnp.where(
                diag_y, jnp.tile(wyc_ref[0, s, part], (grp, 1)), 0.0).astype(wy_ref.dtype)

    def z_body(i, carry):
        rs = pl.ds(pl.multiple_of(i * row_chunk, row_chunk), row_chunk)
        z_ref[rs, :] = jnp.dot(ucat_ref[rs, :], wz_ref[...], preferred_element_type=F32)
        return carry

    lax.fori_loop(0, n_chunks, z_body, 0)

    a_re = jnp.broadcast_to(a_ref[0, 0:1, :], (S5_SEQS, ns))
    a_im = jnp.broadcast_to(a_ref[0, 1:2, :], (S5_SEQS, ns))

    def advance(k, sr, si):
        rs = pl.ds(pl.multiple_of(k * S5_SEQS, S5_SEQS), S5_SEQS)
        zr = z_ref[rs, :ns]
        zi = z_ref[rs, ns:]
        return rs, a_re * sr - a_im * si + zr, a_re * si + a_im * sr + zi

    def pass1(k, st):
        _, nr, ni = advance(k, *st)
        return nr, ni

    zero = jnp.zeros((S5_SEQS, ns), F32)
    end_re, end_im = lax.fori_loop(0, n_sub, pass1, (zero, zero))

    odd = lax.broadcasted_iota(jnp.int32, (S5_SEQS, ns), 0) % 2 == 1
    init = (jnp.where(odd, pltpu.roll(end_re, 1, axis=0), 0.0),
            jnp.where(odd, pltpu.roll(end_im, 1, axis=0), 0.0))

    def pass2(k, st):
        rs, nr, ni = advance(k, *st)
        z_ref[rs, :ns] = st[0]
        z_ref[rs, ns:] = st[1]
        return nr, ni

    lax.fori_loop(0, n_sub, pass2, init)

    def y_body(i, carry):
        rs = pl.ds(pl.multiple_of(i * row_chunk, row_chunk), row_chunk)
        y = (jnp.dot(ucat_ref[rs, :], m_ref[...], preferred_element_type=F32)
             + jnp.dot(z_ref[rs, :].astype(BF16), wy_ref[...], preferred_element_type=F32))
        for s in range(S5_SUB):
            y_ref[0, s, rs, :] = y[:, s * LANES:(s + 1) * LANES]
        return carry

    lax.fori_loop(0, n_chunks, y_body, 0)


def s5_core(u, kc, wzc, wyc, a_sub, *, row_chunk=512, name):
    tiles, sub, rows, _ = u.shape
    ns = a_sub.shape[2]
    lk = sub * LANES
    row_chunk = _fit(rows, row_chunk)
    return pl.pallas_call(
        partial(_s5_kernel, row_chunk=row_chunk),
        out_shape=jax.ShapeDtypeStruct((tiles, sub, rows, LANES), F32),
        grid=(tiles,),
        in_specs=[
            pl.BlockSpec((1, sub, rows, LANES), lambda j: (j, 0, 0, 0)),
            pl.BlockSpec((1, sub, S5_GROUP, LANES), lambda j: (j, 0, 0, 0)),
            pl.BlockSpec((1, sub, 2, S5_GROUP, ns), lambda j: (j, 0, 0, 0, 0)),
            pl.BlockSpec((1, sub, 2, S5_STATE, LANES), lambda j: (j, 0, 0, 0, 0)),
            pl.BlockSpec((1, 2, ns), lambda j: (j, 0, 0)),
        ],
        out_specs=pl.BlockSpec((1, sub, rows, LANES), lambda j: (j, 0, 0, 0)),
        scratch_shapes=[pltpu.VMEM((rows, lk), BF16), pltpu.VMEM((lk, lk), BF16), pltpu.VMEM((lk, 2 * ns), BF16),
                        pltpu.VMEM((2 * ns, lk), BF16), pltpu.VMEM((rows, 2 * ns), F32)],
        compiler_params=_params(("parallel",)),
        name=name,
    )(u, kc, wzc, wyc, a_sub)


def _s5_operators(a_re, a_im, log_dt, b_re, b_im, c_re, c_im):
    g, p = a_re.shape
    c = S5_GROUP
    gpt = LANES // c
    tiles = g // gpt
    ls = S5_SUB

    lam_re, lam_im = a_re.astype(F32), a_im.astype(F32)
    dt = jnp.exp(log_dt.astype(F32))[:, None]
    mag = jnp.exp(lam_re * dt)
    ab_re = mag * jnp.cos(lam_im * dt)
    ab_im = mag * jnp.sin(lam_im * dt)
    den = lam_re * lam_re + lam_im * lam_im
    nr, ni = ab_re - 1.0, ab_im
    coef_re = (nr * lam_re + ni * lam_im) / den
    coef_im = (ni * lam_re - nr * lam_im) / den
    br, bi = b_re.astype(F32), b_im.astype(F32)
    bb_re = coef_re[..., None] * br - coef_im[..., None] * bi
    bb_im = coef_re[..., None] * bi + coef_im[..., None] * br
    cr, ci = c_re.astype(F32), c_im.astype(F32)

    pw_re, pw_im = [jnp.ones_like(ab_re)], [jnp.zeros_like(ab_im)]
    for _ in range(ls):
        r0, i0 = pw_re[-1], pw_im[-1]
        pw_re.append(r0 * ab_re - i0 * ab_im)
        pw_im.append(r0 * ab_im + i0 * ab_re)
    pw_re, pw_im = jnp.stack(pw_re), jnp.stack(pw_im)

    def per_tile(t, lead):
        r, w = t.shape[-2:]
        n = len(lead)
        t = t.reshape(*lead, tiles, gpt, r, w)
        t = jnp.transpose(t, (n, *range(n), n + 2, n + 1, n + 3))
        return t.reshape(tiles, *lead, r, gpt * w)

    cb_re = jnp.einsum('gop,tgp->tgop', cr, pw_re[:ls]) - jnp.einsum('gop,tgp->tgop', ci, pw_im[:ls])
    cb_im = jnp.einsum('gop,tgp->tgop', cr, pw_im[:ls]) + jnp.einsum('gop,tgp->tgop', ci, pw_re[:ls])
    bbt_re, bbt_im = bb_re.transpose(0, 2, 1)[None, :, :, None, :], bb_im.transpose(0, 2, 1)[None, :, :, None, :]
    kern = jnp.sum(cb_re[:, :, None, :, :] * bbt_re - cb_im[:, :, None, :, :] * bbt_im, axis=-1)
    kc = per_tile(kern, (ls,))

    rev_re, rev_im = pw_re[ls - 1::-1], pw_im[ls - 1::-1]
    wz_re = jnp.einsum('sgp,gpi->sgip', rev_re, bb_re) - jnp.einsum('sgp,gpi->sgip', rev_im, bb_im)
    wz_im = jnp.einsum('sgp,gpi->sgip', rev_re, bb_im) + jnp.einsum('sgp,gpi->sgip', rev_im, bb_re)
    wzc = per_tile(jnp.stack([wz_re, wz_im], axis=1), (ls, 2))

    wy_re = jnp.einsum('gop,tgp->tgpo', cr, pw_re[1:]) - jnp.einsum('gop,tgp->tgpo', ci, pw_im[1:])
    wy_im = -(jnp.einsum('gop,tgp->tgpo', cr, pw_im[1:]) + jnp.einsum('gop,tgp->tgpo', ci, pw_re[1:]))
    wyc = per_tile(jnp.stack([wy_re, wy_im], axis=1), (ls, 2))

    a_sub = jnp.stack([pw_re[ls].reshape(tiles, gpt * p), pw_im[ls].reshape(tiles, gpt * p)], axis=1)
    return kc, wzc, wyc, a_sub


def _s5_act_kernel(x_ref, y_ref, g_ref, d_ref, o_ref, slab_ref):
    tt = x_ref.shape[1]
    tiles = y_ref.shape[0]
    nk = tt // S5_SUB
    for q in range(S5_SEQS):
        x = x_ref[q]
        inv_rms = lax.rsqrt(jnp.mean(x * x, axis=-1, keepdims=True) + EPS)
        for j in range(tiles):
            lanes = slice(j * LANES, (j + 1) * LANES)
            for s in range(S5_SUB):
                slab_ref[pl.ds(s, nk, stride=S5_SUB), :] = y_ref[j, s, pl.ds(q, nk, stride=S5_SEQS), :]
            hn = x_ref[q, :, lanes] * inv_rms * g_ref[:, lanes]
            o_ref[q, :, lanes] = jax.nn.gelu(slab_ref[...] + d_ref[:, lanes] * hn).astype(o_ref.dtype)


def s5_activation(x, y_scan, g, d_skip, *, batch, seq, tt=64, name):
    m, d = x.shape
    tiles = d // LANES
    half = seq // 2
    tt = _fit(half, tt)
    act = pl.pallas_call(
        _s5_act_kernel,
        out_shape=jax.ShapeDtypeStruct((S5_SEQS, half, d), BF16),
        grid=(half // tt,),
        in_specs=[
            pl.BlockSpec((S5_SEQS, tt, d), lambda i: (0, i, 0)),
            pl.BlockSpec((tiles, S5_SUB, tt, LANES), lambda i: (0, 0, i, 0)),
            pl.BlockSpec((1, d), lambda i: (0, 0)),
            pl.BlockSpec((1, d), lambda i: (0, 0)),
        ],
        out_specs=pl.BlockSpec((S5_SEQS, tt, d), lambda i: (0, i, 0)),
        scratch_shapes=[pltpu.VMEM((tt, LANES), F32)],
        compiler_params=_params(("parallel",)),
        name=name,
    )(x.reshape(S5_SEQS, half, d), y_scan, g.reshape(1, d), d_skip.reshape(1, d))
    return act.reshape(m, d)


def _glu_residual_kernel(a_ref, wv_ref, wg_ref, x_ref, o_ref):
    a = a_ref[...]
    val = jnp.dot(a, wv_ref[...], preferred_element_type=F32)
    gate = jnp.dot(a, wg_ref[...], preferred_element_type=F32)
    o_ref[...] = x_ref[...] + val * jax.nn.sigmoid(gate)


def glu_residual(a, w_glu, layer, x, *, tm=1024, tn=512, name):
    m, d = x.shape
    tm, tn = _fit(m, tm), _fit(d, tn)
    nb = d // tn
    return pl.pallas_call(
        _glu_residual_kernel,
        out_shape=jax.ShapeDtypeStruct((m, d), F32),
        grid=(m // tm, nb),
        in_specs=[
            pl.BlockSpec((tm, d), lambda i, j: (i, 0)),
            pl.BlockSpec((None, d, tn), lambda i, j: (layer, 0, j)),
            pl.BlockSpec((None, d, tn), lambda i, j: (layer, 0, nb + j)),
            pl.BlockSpec((tm, tn), lambda i, j: (i, j)),
        ],
        out_specs=pl.BlockSpec((tm, tn), lambda i, j: (i, j)),
        compiler_params=_params(("parallel", "arbitrary")),
        name=name,
    )(a, w_glu, w_glu, x)


def s5_mixer_residual(x, g, a_re, a_im, log_dt, b_re, b_im, c_re, c_im, d_skip, w_glu, layer, *, batch, seq,
                      tag):
    u = s5_norm_to_scan_layout(x, g, batch=batch, seq=seq, name=f"s5_norm_{tag}")
    kc, wzc, wyc, a_sub = _s5_operators(a_re, a_im, log_dt, b_re, b_im, c_re, c_im)
    y = s5_core(u, kc, wzc, wyc, a_sub, name=f"s5_scan_{tag}")
    act = s5_activation(x, y, g, d_skip, batch=batch, seq=seq, name=f"s5_act_{tag}")
    return glu_residual(act, w_glu, layer, x, name=f"s5_glu_{tag}")


def kernel(x, norm_mix, norm_mlp, norm_final, a_w_in, a_lambda, a_subln, a_w_out, b_a_re, b_a_im, b_log_dt, b_b_re, b_b_im, b_c_re, b_c_im, b_d, b_w_glu, c_w_in, c_w_out, mlp_w1, mlp_w2):
    batch, seq, d = x.shape
    depth = norm_mix.shape[0]
    assert batch * 2 == S5_SEQS, "S5 scan layout places batch x two time halves on the 8 sublanes"
    xf = x.reshape(batch * seq, d)
    a_w_in, a_w_out, b_w_glu, c_w_in, c_w_out, mlp_w1, mlp_w2 = (
        w.astype(BF16) for w in (a_w_in, a_w_out, b_w_glu, c_w_in, c_w_out, mlp_w1, mlp_w2))

    for i in range(depth):
        kind = i % N_MIXERS
        j = i // N_MIXERS
        if kind == 0:
            lambda_init = 0.8 - 0.6 * math.exp(-0.3 * i)
            qkv = norm_matmul(xf, norm_mix[i], a_w_in, j, scaled_cols=d, scale=ATTN_Q_SCALE, name=f"attn_in_{i}")
            o = diff_attention_core(qkv, a_lambda[j], a_subln[j], batch=batch, seq=seq,
                                    lambda_init=lambda_init, name=f"attn_core_{i}")
            xf = matmul_residual(o, a_w_out, j, xf, tm=512, tn=d, name=f"attn_out_{i}")
        elif kind == 1:
            xf = s5_mixer_residual(xf, norm_mix[i], b_a_re[j], b_a_im[j], b_log_dt[j], b_b_re[j],
                                   b_b_im[j], b_c_re[j], b_c_im[j], b_d[j], b_w_glu, j,
                                   batch=batch, seq=seq, tag=str(i))
        else:
            qk = norm_matmul(xf, norm_mix[i], c_w_in, j, col_start=0, n_cols=2 * d, epilogue="rope",
                             scaled_cols=d, scale=RET_QK_DIM ** -0.5, rope=rope_tables(seq), seq=seq,
                             head_dim=RET_QK_DIM, name=f"ret_in_qk_{i}")
            v = norm_matmul(xf, norm_mix[i], c_w_in, j, col_start=2 * d, n_cols=2 * d, name=f"ret_in_v_{i}")
            gate = norm_matmul(xf, norm_mix[i], c_w_in, j, col_start=4 * d, n_cols=2 * d, epilogue="silu",
                               name=f"ret_in_g_{i}")
            o = retention_core(qk, v, gate, batch=batch, seq=seq, name=f"ret_core_{i}")
            xf = matmul_residual(o, c_w_out, j, xf, name=f"ret_out_{i}")
        xf = mlp_residual(xf, norm_mlp[i], mlp_w1, mlp_w2, i, norm_final,
                          final_norm=(i == depth - 1), name=f"mlp_{i}")
    return xf.reshape(batch, seq, d)
```
